```python
import jax, jax.numpy as jnp
from jax import lax
import numpy as np

D_MODEL = 1024
BATCH = 8
SEQ = 2048
DEPTH = 2

SB_HEADS = 8
SB_HEAD_DIM = 64
SB_WIDTH = SB_HEADS * SB_HEAD_DIM
SB_BLOCK = 128
SSD_WIDTH = D_MODEL
SSD_HEAD_DIM = 64
SSD_HEADS = SSD_WIDTH // SSD_HEAD_DIM
SSD_GROUPS = 2
SSD_STATE = 64
SSD_CONV = 4
SSD_CHUNK = 128
SSD_CONV_CH = SSD_WIDTH + 2 * SSD_GROUPS * SSD_STATE
RW_HEAD_DIM = 64
RW_WIDTH = D_MODEL // 2
RW_HEADS = RW_WIDTH // RW_HEAD_DIM
RW_DECAY_RANK = 64
RW_ICLR_RANK = 64
N_BRANCH = 3
SB_COLS = 4 * SB_WIDTH
SSD_COLS = SSD_WIDTH + SSD_CONV_CH + SSD_HEADS
RW_COLS = 4 * RW_WIDTH + RW_DECAY_RANK + RW_ICLR_RANK
GATE_COLS = N_BRANCH * D_MODEL
N_IN = SB_COLS + SSD_COLS + RW_COLS + GATE_COLS
RMS_EPS = 1e-6
GN_EPS = 64e-5

kernel_name = "hybrid_sba_ssd_rwkv7_gated"


def _split(x, sizes):
    idx = np.cumsum(sizes)[:-1].tolist()
    return jnp.split(x, idx, axis=-1)


def rms_norm(x, g):
    xf = x.astype(jnp.float32)
    y = xf * lax.rsqrt(jnp.mean(xf * xf, axis=-1, keepdims=True) + RMS_EPS)
    return (y * g.astype(jnp.float32)).astype(x.dtype)


def stick_breaking_attention(q, k, v):
    b, s, h, dh = q.shape
    scale = dh ** -0.5
    outs = []
    for i in range(s // SB_BLOCK):
        q0 = i * SB_BLOCK
        end = q0 + SB_BLOCK
        qb, kb, vb = q[:, q0:end], k[:, :end], v[:, :end]
        z = jnp.einsum('bqhd,bkhd->bhqk', qb, kb).astype(jnp.float32) * scale
        t_idx = q0 + jnp.arange(SB_BLOCK)
        s_idx = jnp.arange(end)
        mask = s_idx[None, :] < t_idx[:, None]
        log_beta = jax.nn.log_sigmoid(z)
        log_keep = jnp.where(mask, jax.nn.log_sigmoid(-z), 0.0)
        after = lax.cumsum(log_keep, axis=3, reverse=True) - log_keep
        att = jnp.where(mask, jnp.exp(log_beta + after), 0.0)
        outs.append(jnp.einsum('bhqk,bkhd->bqhd', att.astype(v.dtype), vb))
    return jnp.concatenate(outs, axis=1).reshape(b, s, h * dh)


def causal_depthwise_conv(x, w, bias):
    kw, ch = w.shape
    y = lax.conv_general_dilated(x, w[:, None, :], window_strides=(1,), padding=[(kw - 1, 0)],
                                 dimension_numbers=('NWC', 'WIO', 'NWC'), feature_group_count=ch)
    return y + bias


def segsum(a):
    t = a.shape[-1]
    rep = jnp.broadcast_to(a[..., None], a.shape + (t,))
    strict = jnp.tril(jnp.ones((t, t), dtype=bool), -1)
    cs = jnp.cumsum(jnp.where(strict, rep, 0), axis=-2)
    return jnp.where(jnp.tril(jnp.ones((t, t), dtype=bool)), cs, -jnp.inf)


def ssd_mixer(xbc_raw, dt_raw, conv_w, conv_b, dt_bias, a_log, d_skip):
    b, s, _ = xbc_raw.shape
    c, l, g = s // SSD_CHUNK, SSD_CHUNK, SSD_GROUPS
    j, p, n = SSD_HEADS // SSD_GROUPS, SSD_HEAD_DIM, SSD_STATE
    xbc = jax.nn.silu(causal_depthwise_conv(xbc_raw, conv_w, conv_b))
    xs, bm, cm = _split(xbc, [SSD_WIDTH, g * n, g * n])
    xs = xs.reshape(b, c, l, g, j, p)
    bm = bm.reshape(b, c, l, g, n)
    cm = cm.reshape(b, c, l, g, n)
    dt = jax.nn.softplus(dt_raw + dt_bias).reshape(b, c, l, g, j)
    a_head = -jnp.exp(a_log).reshape(g, j)
    da = jnp.moveaxis(dt * a_head, 2, -1)
    x_dt = xs * dt[..., None]
    a_cs = jnp.cumsum(da, axis=-1)
    decay_in = jnp.exp(segsum(da))
    cb = jnp.einsum('bclgn,bcsgn->bcgls', cm, bm)
    y_diag = jnp.einsum('bcgls,bcgjls,bcsgjp->bclgjp', cb, decay_in, x_dt)
    decay_states = jnp.exp(a_cs[..., -1:] - a_cs)
    states = jnp.einsum('bclgn,bcgjl,bclgjp->bcgjpn', bm, decay_states, x_dt)
    last = jnp.pad(jnp.moveaxis(a_cs[..., -1], 1, -1), [(0, 0), (0, 0), (0, 0), (1, 0)])
    decay_chunk = jnp.exp(segsum(last))
    states_p = jnp.concatenate([jnp.zeros_like(states[:, :1]), states], axis=1)
    new_states = jnp.einsum('bgjzc,bcgjpn->bzgjpn', decay_chunk, states_p)
    prev_states = new_states[:, :-1]
    y_off = jnp.einsum('bclgn,bcgjpn,bcgjl->bclgjp', cm, prev_states, jnp.exp(a_cs))
    y = y_diag + y_off + xs * d_skip.reshape(g, j)[:, :, None]
    return y.reshape(b, s, SSD_WIDTH)


def _rwkv7_step(state, inp):
    r_t, w_t, k_t, v_t, kk_t, a_t = inp
    sa = jnp.einsum('bhij,bhj->bhi', state, -kk_t)
    state = (state * w_t[:, :, None, :] + sa[..., None] * (kk_t * a_t)[:, :, None, :]
             + v_t[..., None] * k_t[:, :, None, :])
    y_t = jnp.einsum('bhij,bhj->bhi', state, r_t)
    return state, y_t


def rwkv7_mixer(slab, mu, w0, w_up, a0, a_up, k_k, k_a, r_k, ln_g, ln_b):
    b, s, _ = slab.shape
    hh, nn = RW_HEADS, RW_HEAD_DIM
    prev = jnp.pad(slab[:, :-1], [(0, 0), (1, 0), (0, 0)])
    mixed = slab + (prev - slab) * mu
    r, k, v, gate, w_lo, a_lo = _split(mixed, [RW_WIDTH] * 4 + [RW_DECAY_RANK, RW_ICLR_RANK])
    w = -jax.nn.softplus(-(w0 + jnp.tanh(w_lo) @ w_up)) - 0.5
    decay = jnp.exp(-jnp.exp(w.astype(jnp.float32)))
    a = jax.nn.sigmoid(a0 + a_lo @ a_up)
    kk = (k * k_k).reshape(b, s, hh, nn).astype(jnp.float32)
    kk = kk / jnp.maximum(jnp.sqrt(jnp.sum(kk * kk, axis=-1, keepdims=True)), 1e-12)
    k = k * (1 + (a - 1) * k_a)
    heads = lambda t: t.reshape(b, s, hh, nn).astype(jnp.float32)
    r4, k4, v4, a4, w4 = heads(r), heads(k), heads(v), heads(a), heads(decay)
    seq_first = lambda t: jnp.moveaxis(t, 1, 0)
    state0 = jnp.zeros((b, hh, nn, nn), jnp.float32)
    _, ys = lax.scan(_rwkv7_step, state0,
                     (seq_first(r4), seq_first(w4), seq_first(k4), seq_first(v4), seq_first(kk), seq_first(a4)))
    y = jnp.moveaxis(ys, 0, 1)
    mean = jnp.mean(y, axis=-1, keepdims=True)
    var = jnp.mean(jnp.square(y - mean), axis=-1, keepdims=True)
    y = ((y - mean) * lax.rsqrt(var + GN_EPS)).reshape(b, s, RW_WIDTH) * ln_g + ln_b
    bonus = jnp.sum(r4 * k4 * r_k, axis=-1, keepdims=True) * v4
    y = y + bonus.reshape(b, s, RW_WIDTH)
    return y.astype(slab.dtype), gate


def hybrid_layer(x, norm_g, w_in, conv_w, conv_b, dt_bias, a_log, d_skip, ssd_norm_g,
                 rw_mu, rw_w0, rw_w_up, rw_a0, rw_a_up, rw_k_k, rw_k_a, rw_r_k, rw_ln_g, rw_ln_b,
                 w_out_sb, w_out_ssd, w_out_rw, w_o):
    b, s, _ = x.shape
    h = rms_norm(x, norm_g)
    proj = h @ w_in
    sb_cols, ssd_cols, rw_cols, gate_cols = _split(proj, [SB_COLS, SSD_COLS, RW_COLS, GATE_COLS])
    q, k, v, sb_gate = _split(sb_cols, [SB_WIDTH] * 4)
    shp = (b, s, SB_HEADS, SB_HEAD_DIM)
    y_sb = stick_breaking_attention(q.reshape(shp), k.reshape(shp), v.reshape(shp)) * jax.nn.silu(sb_gate)
    z, xbc, dt_raw = _split(ssd_cols, [SSD_WIDTH, SSD_CONV_CH, SSD_HEADS])
    y_ssd = ssd_mixer(xbc, dt_raw, conv_w, conv_b, dt_bias, a_log, d_skip)
    y_ssd = rms_norm(y_ssd * jax.nn.silu(z), ssd_norm_g)
    y_rw, rw_gate = rwkv7_mixer(rw_cols, rw_mu, rw_w0, rw_w_up, rw_a0, rw_a_up, rw_k_k, rw_k_a,
                                rw_r_k, rw_ln_g, rw_ln_b)
    y_rw = y_rw * jax.nn.silu(rw_gate)
    g_sb, g_ssd, g_rw = _split(jax.nn.sigmoid(gate_cols), [D_MODEL] * N_BRANCH)
    merged = g_sb * (y_sb @ w_out_sb) + g_ssd * (y_ssd @ w_out_ssd) + g_rw * (y_rw @ w_out_rw)
    return x + merged @ w_o


def setup_inputs(seed: int = 0) -> dict:
    key = jax.random.key(seed)
    ks = jax.random.split(key, 24)
    f32 = jnp.float32
    nrm = lambda k, shp, sc: jax.random.normal(k, shp, f32) * sc
    dt0 = jnp.exp(jax.random.uniform(ks[5], (DEPTH, SSD_HEADS), f32, np.log(1e-3), np.log(1e-1)))
    return {
        "x": nrm(ks[0], (BATCH, SEQ, D_MODEL), 1.0),
        "norm_g": 1.0 + nrm(ks[1], (DEPTH, D_MODEL), 0.02),
        "w_in": nrm(ks[2], (DEPTH, D_MODEL, N_IN), D_MODEL ** -0.5),
        "conv_w": nrm(ks[3], (DEPTH, SSD_CONV, SSD_CONV_CH), SSD_CONV ** -0.5),
        "conv_b": nrm(ks[4], (DEPTH, SSD_CONV_CH), 0.02),
        "dt_bias": dt0 + jnp.log(-jnp.expm1(-dt0)),
        "a_log": jnp.log(jax.random.uniform(ks[6], (DEPTH, SSD_HEADS), f32, 1.0, 16.0)),
        "d_skip": 1.0 + nrm(ks[7], (DEPTH, SSD_HEADS), 0.02),
        "ssd_norm_g": 1.0 + nrm(ks[8], (DEPTH, SSD_WIDTH), 0.02),
        "rw_mu": jax.random.uniform(ks[9], (DEPTH, RW_COLS), f32, 0.0, 1.0),
        "rw_w0": jax.random.uniform(ks[10], (DEPTH, RW_WIDTH), f32, -6.0, -1.0),
        "rw_w_up": nrm(ks[11], (DEPTH, RW_DECAY_RANK, RW_WIDTH), 0.1),
        "rw_a0": nrm(ks[12], (DEPTH, RW_WIDTH), 0.1),
        "rw_a_up": nrm(ks[13], (DEPTH, RW_ICLR_RANK, RW_WIDTH), 0.1),
        "rw_k_k": 0.85 + nrm(ks[14], (DEPTH, RW_WIDTH), 0.02),
        "rw_k_a": 1.0 + nrm(ks[15], (DEPTH, RW_WIDTH), 0.02),
        "rw_r_k": nrm(ks[16], (DEPTH, RW_HEADS, RW_HEAD_DIM), 0.1),
        "rw_ln_g": 1.0 + nrm(ks[17], (DEPTH, RW_WIDTH), 0.02),
        "rw_ln_b": nrm(ks[18], (DEPTH, RW_WIDTH), 0.02),
        "w_out_sb": nrm(ks[19], (DEPTH, SB_WIDTH, D_MODEL), SB_WIDTH ** -0.5),
        "w_out_ssd": nrm(ks[20], (DEPTH, SSD_WIDTH, D_MODEL), SSD_WIDTH ** -0.5),
        "w_out_rw": nrm(ks[21], (DEPTH, RW_WIDTH, D_MODEL), RW_WIDTH ** -0.5),
        "w_o": nrm(ks[22], (DEPTH, D_MODEL, D_MODEL), D_MODEL ** -0.5),
        "final_g": 1.0 + nrm(ks[23], (D_MODEL,), 0.02),
    }


def reference(x, norm_g, w_in, conv_w, conv_b, dt_bias, a_log, d_skip, ssd_norm_g,
              rw_mu, rw_w0, rw_w_up, rw_a0, rw_a_up, rw_k_k, rw_k_a, rw_r_k, rw_ln_g, rw_ln_b,
              w_out_sb, w_out_ssd, w_out_rw, w_o, final_g):
    for i in range(DEPTH):
        x = hybrid_layer(x, norm_g[i], w_in[i], conv_w[i], conv_b[i], dt_bias[i], a_log[i], d_skip[i],
                         ssd_norm_g[i], rw_mu[i], rw_w0[i], rw_w_up[i], rw_a0[i], rw_a_up[i], rw_k_k[i],
                         rw_k_a[i], rw_r_k[i], rw_ln_g[i], rw_ln_b[i], w_out_sb[i], w_out_ssd[i],
                         w_out_rw[i], w_o[i])
    return rms_norm(x, final_g)
```

```python
import functools

import jax
import jax.numpy as jnp
from jax import lax
from jax.experimental import pallas as pl
from jax.experimental.pallas import tpu as pltpu

F32 = jnp.float32
BF16 = jnp.bfloat16

D_MODEL = 1024
HEAD = 64
LANES = 128
SB_WIDTH = 512
SSD_WIDTH = 1024
SSD_HEADS = 16
SSD_GROUPS = 2
SSD_STATE = 64
SSD_CONV = 4
RW_WIDTH = 512
RW_LORA = 128
RMS_EPS = 1e-6
GN_EPS = 64e-5

COL_SB = 0
COL_RW = 2048
COL_Z = 4096
COL_X = 5120
COL_GATES = 6144
COL_BC = 9216
COL_LORA = 9472
COL_DT = 9600
N_PACK = 9728

VMEM_LIMIT = 48 * 1024 * 1024


def _dot(a, b):
    return jnp.dot(a, b, preferred_element_type=F32)


def _dot_nt(a, b):
    return lax.dot_general(a, b, (((1,), (1,)), ((), ())), preferred_element_type=F32)


def _dot_tn(a, b):
    return lax.dot_general(a, b, (((0,), (0,)), ((), ())), preferred_element_type=F32)


def _split(x, parts):
    out = []
    for _ in range(parts - 1):
        p = x.astype(BF16)
        out.append(p)
        x = x - p.astype(F32)
    out.append(x.astype(BF16))
    return out


def _dot_exact_rhs(a, b_bf16, parts=3):
    acc = None
    for p in _split(a, parts):
        t = _dot(p, b_bf16)
        acc = t if acc is None else acc + t
    return acc


def _dot_exact_lhs(a_bf16, b, parts=3):
    acc = None
    for p in _split(b, parts):
        t = _dot(a_bf16, p)
        acc = t if acc is None else acc + t
    return acc


def _softplus(x):
    return jnp.maximum(x, 0.0) + jnp.log1p(jnp.exp(-jnp.abs(x)))


def _sigmoid(x):
    return 1.0 / (1.0 + jnp.exp(-x))


def _silu(x):
    return x * _sigmoid(x)


def _rms_rows(x, g):
    ms = jnp.mean(x * x, axis=-1, keepdims=True)
    return x * lax.rsqrt(ms + RMS_EPS) * g


def _pack_in_cols(w):
    pad = jnp.zeros(w.shape[:-1] + (N_PACK - COL_DT - SSD_HEADS,), w.dtype)
    return jnp.concatenate([
        w[..., 0:2048],
        w[..., 4368:6416],
        w[..., 2048:3072],
        w[..., 3072:4096],
        w[..., 6544:9616],
        w[..., 4096:4352],
        w[..., 6416:6544],
        w[..., 4352:4368],
        pad], axis=-1)


def _inproj_kernel(x_ref, g_ref, w_ref, o_ref, h_ref, *, rows):
    @pl.when(pl.program_id(1) == 0)
    def _():
        def body(r, c):
            sl = pl.ds(pl.multiple_of(r * rows, rows), rows)
            h_ref[sl, :] = _rms_rows(x_ref[sl, :], g_ref[...]).astype(BF16)
            return c
        lax.fori_loop(0, x_ref.shape[0] // rows, body, 0)

    o_ref[...] = _dot(h_ref[...], w_ref[...])


def _inproj(x2, g, w_pack, *, tm=1024, tn=512):
    t, d = x2.shape
    n = w_pack.shape[1]
    return pl.pallas_call(
        functools.partial(_inproj_kernel, rows=min(tm, 256)),
        out_shape=jax.ShapeDtypeStruct((t, n), F32),
        grid=(t // tm, n // tn),
        in_specs=[
            pl.BlockSpec((tm, d), lambda i, j: (i, 0)),
            pl.BlockSpec((1, d), lambda i, j: (0, 0)),
            pl.BlockSpec((d, tn), lambda i, j: (0, j)),
        ],
        out_specs=pl.BlockSpec((tm, tn), lambda i, j: (i, j)),
        scratch_shapes=[pltpu.VMEM((tm, d), BF16)],
        compiler_params=pltpu.CompilerParams(
            dimension_semantics=("parallel", "arbitrary"), vmem_limit_bytes=VMEM_LIMIT),
        name="inproj",
    )(x2, g.reshape(1, d), w_pack)


def _sb_kernel(q_ref, k_ref, v_ref, gate_ref, o_ref, kb_ref, vb_ref, *, blk):
    qi = pl.program_id(2)

    @pl.when(qi == 0)
    def _():
        kb_ref[...] = k_ref[...].astype(BF16)
        vb_ref[...] = v_ref[...].astype(BF16)

    lane = lax.broadcasted_iota(jnp.int32, (blk, LANES), 1)
    first = lane < HEAD
    q = q_ref[...] * (HEAD ** -0.5)
    q_heads = (jnp.where(first, q, 0.0).astype(BF16), jnp.where(first, 0.0, q).astype(BF16))
    row = lax.broadcasted_iota(jnp.int32, (blk, blk), 0)
    col = lax.broadcasted_iota(jnp.int32, (blk, blk), 1)
    strict = col < row
    suffix = (row > col).astype(BF16)

    def block(j, carry, masked):
        ks = pl.ds(pl.multiple_of(j * blk, blk), blk)
        kj = kb_ref[ks, :]
        vj = vb_ref[ks, :]
        out = []
        for h in range(2):
            acc, run = carry[h]
            z = _dot_nt(q_heads[h], kj)
            l = jnp.log1p(jnp.exp(-jnp.abs(z)))
            log_beta = jnp.minimum(z, 0.0) - l
            log_keep = -jnp.maximum(z, 0.0) - l
            if masked:
                log_keep = jnp.where(strict, log_keep, 0.0)
            after = _dot_exact_rhs(log_keep, suffix, parts=2) + run
            att = jnp.exp(log_beta + after)
            if masked:
                att = jnp.where(strict, att, 0.0)
            acc = acc + _dot(att.astype(BF16), vj)
            run = run + jnp.sum(log_keep, axis=-1, keepdims=True)
            out.append((acc, run))
        return tuple(out)

    zero = (jnp.zeros((blk, LANES), F32), jnp.zeros((blk, 1), F32))
    carry = block(qi, (zero, zero), True)
    carry = lax.fori_loop(0, qi, lambda i, c: block(qi - 1 - i, c, False), carry)
    y = jnp.where(first, carry[0][0], carry[1][0])
    o_ref[...] = (y * _silu(gate_ref[...])).astype(BF16)


def _sb_attention(proj, batch, seq, *, blk=128):
    nq = seq // blk
    pairs = SB_WIDTH // LANES
    c0 = COL_SB // LANES
    return pl.pallas_call(
        functools.partial(_sb_kernel, blk=blk),
        out_shape=jax.ShapeDtypeStruct((batch * seq, SB_WIDTH), BF16),
        grid=(batch, pairs, nq),
        in_specs=[
            pl.BlockSpec((blk, LANES), lambda b, p, i: (b * nq + i, c0 + p)),
            pl.BlockSpec((seq, LANES), lambda b, p, i: (b, c0 + pairs + p)),
            pl.BlockSpec((seq, LANES), lambda b, p, i: (b, c0 + 2 * pairs + p)),
            pl.BlockSpec((blk, LANES), lambda b, p, i: (b * nq + i, c0 + 3 * pairs + p)),
        ],
        out_specs=pl.BlockSpec((blk, LANES), lambda b, p, i: (b * nq + i, p)),
        scratch_shapes=[pltpu.VMEM((seq, LANES), BF16), pltpu.VMEM((seq, LANES), BF16)],
        compiler_params=pltpu.CompilerParams(
            dimension_semantics=("parallel", "parallel", "arbitrary"), vmem_limit_bytes=VMEM_LIMIT),
        name="sb_attention",
    )(proj, proj, proj, proj)


def _ssd_kernel(z_ref, x_ref, bc_ref, dt_ref, cwx_ref, cwbc_ref, cbx_ref, cbbc_ref, dtb_ref, ah_ref,
                dskip_ref, ng_ref, tri_ref, e1_ref, e2_ref, o_ref, xpad_ref, bcpad_ref, state_ref, *, chunk):
    ci = pl.program_id(1)
    halo = 8

    @pl.when(ci == 0)
    def _():
        state_ref[...] = jnp.zeros_like(state_ref)
        xpad_ref[0:halo, :] = jnp.zeros((halo, xpad_ref.shape[1]), F32)
        bcpad_ref[0:halo, :] = jnp.zeros((halo, bcpad_ref.shape[1]), F32)

    xpad_ref[halo:halo + chunk, :] = x_ref[...]
    bcpad_ref[halo:halo + chunk, :] = bc_ref[...]

    def conv(pad_ref, w_ref, b_ref):
        acc = b_ref[...]
        for k in range(SSD_CONV):
            off = halo - (SSD_CONV - 1) + k
            acc = acc + w_ref[k:k + 1, :] * pad_ref[off:off + chunk, :]
        return _silu(acc)

    xs = conv(xpad_ref, cwx_ref, cbx_ref)
    bc = conv(bcpad_ref, cwbc_ref, cbbc_ref)
    xpad_ref[0:halo, :] = xpad_ref[chunk:chunk + halo, :]
    bcpad_ref[0:halo, :] = bcpad_ref[chunk:chunk + halo, :]
    bm = bc[:, :LANES]
    cm = bc[:, LANES:]

    dt = _softplus(dt_ref[...] + dtb_ref[...])
    da = dt * ah_ref[...]
    acs = _dot_exact_lhs(tri_ref[...], da)
    e1 = e1_ref[...]
    dt_e = _dot_exact_rhs(dt, e1)
    acs_e = _dot_exact_rhs(acs, e1)
    acs_e2 = _dot_exact_rhs(acs, e2_ref[...])
    acs_t = acs.T

    x_dt = xs * dt_e
    last = acs_e[chunk - 1:chunk, :]
    xs_dec = (x_dt * jnp.exp(last - acs_e)).astype(BF16)

    lane = lax.broadcasted_iota(jnp.int32, (chunk, LANES), 1)
    first = lane < HEAD
    row = lax.broadcasted_iota(jnp.int32, (chunk, chunk), 0)
    col = lax.broadcasted_iota(jnp.int32, (chunk, chunk), 1)
    causal = col <= row
    bm16 = bm.astype(BF16)
    cb = [_dot_nt(jnp.where(first, cm, 0.0).astype(BF16), bm16),
          _dot_nt(jnp.where(first, 0.0, cm).astype(BF16), bm16)]
    x_dt16 = x_dt.astype(BF16)

    heads_per_group = SSD_HEADS // SSD_GROUPS
    pieces = []
    for pair in range(SSD_HEADS // 2):
        xp = x_dt16[:, pair * LANES:(pair + 1) * LANES]
        yd = []
        for h in (2 * pair, 2 * pair + 1):
            seg = acs_e2[:, h * LANES:(h + 1) * LANES] - acs_t[h:h + 1, :]
            m = jnp.where(causal, jnp.exp(seg), 0.0) * cb[h // heads_per_group]
            yd.append(_dot(m.astype(BF16), xp))
        pieces.append(jnp.where(first, yd[0], yd[1]))
    y_diag = jnp.concatenate(pieces, axis=1)

    state = state_ref[...]
    y_off = _dot(cm.astype(BF16), state.astype(BF16)) * jnp.exp(acs_e)
    srow = lax.broadcasted_iota(jnp.int32, state.shape, 0)
    slane = lax.broadcasted_iota(jnp.int32, state.shape, 1)
    same_group = (srow < SSD_STATE) == (slane < SSD_WIDTH // SSD_GROUPS)
    state_ref[...] = state * jnp.exp(last) + jnp.where(same_group, _dot_tn(bm16, xs_dec), 0.0)

    y = y_diag + y_off + xs * dskip_ref[...]
    y = y * _silu(z_ref[...])
    o_ref[...] = _rms_rows(y, ng_ref[...]).astype(BF16)


def _ssd(proj, batch, seq, conv_w, conv_b, dt_bias, a_log, d_skip, norm_g, *, chunk=128):
    nc = seq // chunk
    nbc = 2 * SSD_GROUPS * SSD_STATE
    pad16 = lambda v: jnp.pad(v, (0, LANES - SSD_HEADS)).reshape(1, LANES)
    idx = jnp.arange(chunk)
    tri = (idx[None, :] <= idx[:, None]).astype(BF16)
    hidx = jnp.arange(LANES)[:, None]
    e1 = (jnp.arange(SSD_WIDTH)[None, :] // HEAD == hidx).astype(BF16)
    e2 = (jnp.arange(SSD_HEADS * LANES)[None, :] // LANES == hidx).astype(BF16)
    full = lambda shape: pl.BlockSpec(shape, lambda b, c: (0, 0))
    row_blk = lambda width, colblk: pl.BlockSpec((chunk, width), lambda b, c: (b * nc + c, colblk))
    return pl.pallas_call(
        functools.partial(_ssd_kernel, chunk=chunk),
        out_shape=jax.ShapeDtypeStruct((batch * seq, SSD_WIDTH), BF16),
        grid=(batch, nc),
        in_specs=[
            row_blk(SSD_WIDTH, COL_Z // SSD_WIDTH),
            row_blk(SSD_WIDTH, COL_X // SSD_WIDTH),
            row_blk(nbc, COL_BC // nbc),
            row_blk(LANES, COL_DT // LANES),
            full((SSD_CONV, SSD_WIDTH)), full((SSD_CONV, nbc)),
            full((1, SSD_WIDTH)), full((1, nbc)),
            full((1, LANES)), full((1, LANES)),
            full((1, SSD_WIDTH)), full((1, SSD_WIDTH)),
            full((chunk, chunk)), full((LANES, SSD_WIDTH)), full((LANES, SSD_HEADS * LANES)),
        ],
        out_specs=pl.BlockSpec((chunk, SSD_WIDTH), lambda b, c: (b * nc + c, 0)),
        scratch_shapes=[
            pltpu.VMEM((chunk + 8, SSD_WIDTH), F32),
            pltpu.VMEM((chunk + 8, nbc), F32),
            pltpu.VMEM((LANES, SSD_WIDTH), F32),
        ],
        compiler_params=pltpu.CompilerParams(
            dimension_semantics=("parallel", "arbitrary"), vmem_limit_bytes=VMEM_LIMIT),
        name="ssd",
    )(proj, proj, proj, proj,
      conv_w[:, :SSD_WIDTH], conv_w[:, SSD_WIDTH:],
      conv_b[:SSD_WIDTH].reshape(1, -1), conv_b[SSD_WIDTH:].reshape(1, -1),
      pad16(dt_bias), pad16(-jnp.exp(a_log)),
      jnp.repeat(d_skip, HEAD).reshape(1, -1), norm_g.reshape(1, -1),
      tri, e1, e2)


def _merge_kernel(x_ref, ysb_ref, yssd_ref, yrw_ref, gates_ref, wsb_ref, wssd_ref, wrw_ref, wo_ref, fg_ref,
                  o_ref, *, final_norm):
    g = _sigmoid(gates_ref[...])
    merged = (g[:, :D_MODEL] * _dot(ysb_ref[...], wsb_ref[...])
              + g[:, D_MODEL:2 * D_MODEL] * _dot(yssd_ref[...], wssd_ref[...])
              + g[:, 2 * D_MODEL:] * _dot(yrw_ref[...], wrw_ref[...]))
    out = x_ref[...] + _dot(merged.astype(BF16), wo_ref[...])
    if final_norm:
        out = _rms_rows(out, fg_ref[...])
    o_ref[...] = out


def _merge(x2, y_sb, y_ssd, y_rw, proj, w_sb, w_ssd, w_rw, w_o, final_g, *, final_norm, tm=512):
    t, d = x2.shape
    rows = lambda width, colblk=0: pl.BlockSpec((tm, width), lambda i: (i, colblk))
    full = lambda shape: pl.BlockSpec(shape, lambda i: (0, 0))
    return pl.pallas_call(
        functools.partial(_merge_kernel, final_norm=final_norm),
        out_shape=jax.ShapeDtypeStruct((t, d), F32),
        grid=(t // tm,),
        in_specs=[
            rows(d), rows(SB_WIDTH), rows(SSD_WIDTH), rows(RW_WIDTH),
            rows(3 * d, COL_GATES // (3 * d)),
            full((SB_WIDTH, d)), full((SSD_WIDTH, d)), full((RW_WIDTH, d)), full((d, d)), full((1, d)),
        ],
        out_specs=rows(d),
        compiler_params=pltpu.CompilerParams(
            dimension_semantics=("parallel",), vmem_limit_bytes=VMEM_LIMIT),
        name="merge",
    )(x2, y_sb, y_ssd, y_rw, proj, w_sb, w_ssd, w_rw, w_o, final_g.reshape(1, d))


def _dot2(a, b, dims=None):
    a_hi, a_lo = _split(a, 2)
    b_hi, b_lo = _split(b, 2)
    f = _dot if dims is None else dims
    return f(a_hi, b_hi) + f(a_hi, b_lo) + f(a_lo, b_hi)


def _rwkv_kernel(rkvg_ref, lora_ref, mu_ref, mul_ref, w0_ref, wup_ref, a0_ref, aup_ref, kk_ref, ka_ref, rk_ref,
                 lng_ref, lnb_ref, tri_ref, hsum_ref, o_ref, pad_ref, padl_ref, state_ref, y_ref, *, rows, chunk):
    halo = 8

    @pl.when(pl.program_id(1) == 0)
    def _():
        state_ref[...] = jnp.zeros_like(state_ref)
        pad_ref[0:halo, :] = jnp.zeros((halo, pad_ref.shape[1]), F32)
        padl_ref[0:halo, :] = jnp.zeros((halo, padl_ref.shape[1]), F32)

    def shift_mix(ref, pad, mu):
        cur = ref[...]
        pad[halo:halo + rows, :] = cur
        prev = pad[halo - 1:halo - 1 + rows, :]
        pad[0:halo, :] = pad[rows:rows + halo, :]
        return cur + (prev - cur) * mu[...]

    mixed = shift_mix(rkvg_ref, pad_ref, mu_ref)
    lora = shift_mix(lora_ref, padl_ref, mul_ref)
    w = RW_WIDTH
    r, k, v, gate = (mixed[:, i * w:(i + 1) * w] for i in range(4))
    hsum = hsum_ref[...]

    w_raw = -_softplus(-(w0_ref[...] + _dot2(jnp.tanh(lora), wup_ref[...]))) - 0.5
    log_w = -jnp.exp(w_raw)
    a = _sigmoid(a0_ref[...] + _dot2(lora, aup_ref[...]))
    kk = k * kk_ref[...]
    kk = kk / jnp.maximum(jnp.sqrt(_dot_exact_rhs(kk * kk, hsum)), 1e-12)
    k = k * (1.0 + (a - 1.0) * ka_ref[...])
    bonus = _dot_exact_rhs(r * k * rk_ref[...], hsum) * v
    b = kk * a

    lane = lax.broadcasted_iota(jnp.int32, (chunk, LANES), 1)
    first = lane < HEAD
    n = 2 * chunk
    srow = lax.broadcasted_iota(jnp.int32, (n, n), 0)
    scol = lax.broadcasted_iota(jnp.int32, (n, n), 1)
    strict = scol < srow
    causal = scol <= srow
    eye = (srow == scol).astype(F32)
    drow = lax.broadcasted_iota(jnp.int32, (LANES, LANES), 0)
    dcol = lax.broadcasted_iota(jnp.int32, (LANES, LANES), 1)
    diag = drow == dcol

    def stack(x):
        return jnp.concatenate([jnp.where(first, x, 0.0), jnp.where(first, 0.0, x)], axis=0)

    for c in range(rows // chunk):
        rs = slice(c * chunk, (c + 1) * chunk)
        lw = log_w[rs]
        cl = _dot_exact_lhs(tri_ref[...], lw)
        last = cl[chunk - 1:chunk, :]
        c_incl = jnp.exp(cl)
        c_prev = jnp.exp(cl - lw)
        c_inv = jnp.exp(-cl)
        c_tail = jnp.exp(last - cl)
        a_t = -kk[rs] * c_prev
        r_t = r[rs] * c_incl
        b_t = b[rs] * c_inv
        k_t = k[rs] * c_inv
        b_h = b[rs] * c_tail
        k_h = k[rs] * c_tail
        c_last = jnp.exp(last)
        for p in range(RW_WIDTH // LANES):
            ls = slice(p * LANES, (p + 1) * LANES)
            a_s, r_s, b_s, k_s = stack(a_t[:, ls]), stack(r_t[:, ls]), stack(b_t[:, ls]), stack(k_t[:, ls])
            v_s, bh_s, kh_s = stack(v[rs, ls]), stack(b_h[:, ls]), stack(k_h[:, ls])
            a_ab = jnp.where(strict, _dot2(a_s, b_s, _dot_nt), 0.0)
            a_ak = jnp.where(strict, _dot2(a_s, k_s, _dot_nt), 0.0)
            a_rb = jnp.where(causal, _dot2(r_s, b_s, _dot_nt), 0.0)
            a_rk = jnp.where(causal, _dot2(r_s, k_s, _dot_nt), 0.0)
            x = a_ab
            t = eye + x
            m = 2
            while m < chunk:
                x = _dot2(x, x)
                t = t + _dot2(t, x)
                m *= 2
            p1 = _dot2(t, a_s)
            q1 = _dot2(t, _dot2(a_ak, v_s))
            p2 = r_s + _dot2(a_rb, p1)
            q2 = _dot2(a_rb, q1) + _dot2(a_rk, v_s)
            g = jnp.where(diag, c_last[:, ls], 0.0) + _dot2(bh_s, p1, _dot_tn)
            h = _dot2(bh_s, q1, _dot_tn) + _dot2(kh_s, v_s, _dot_tn)
            st = state_ref[p]
            ys = _dot2(p2, st) + q2
            y_ref[rs, ls] = ys[:chunk] + ys[chunk:]
            state_ref[p] = _dot2(g, st) + h

    y = y_ref[...]
    inv = 1.0 / HEAD
    mean = _dot_exact_rhs(y, hsum) * inv
    d = y - mean
    var = _dot_exact_rhs(d * d, hsum) * inv
    y = d * lax.rsqrt(var + GN_EPS) * lng_ref[...] + lnb_ref[...] + bonus
    o_ref[...] = (y * _silu(gate)).astype(BF16)


def _rwkv(proj, batch, seq, mu, w0, w_up, a0, a_up, k_k, k_a, r_k, ln_g, ln_b, *, rows=256, chunk=64):
    nb = seq // rows
    w4 = 4 * RW_WIDTH
    half = RW_LORA // 2
    idx = jnp.arange(chunk)
    tri = (idx[None, :] <= idx[:, None]).astype(BF16)
    lanes = jnp.arange(RW_WIDTH)
    hsum = (lanes[:, None] // HEAD == lanes[None, :] // HEAD).astype(BF16)
    wup_pad = jnp.concatenate([w_up, jnp.zeros((half, RW_WIDTH), F32)], axis=0)
    aup_pad = jnp.concatenate([jnp.zeros((half, RW_WIDTH), F32), a_up], axis=0)
    vec = lambda a: a.reshape(1, -1)
    full = lambda shape: pl.BlockSpec(shape, lambda b, c: (0, 0))
    return pl.pallas_call(
        functools.partial(_rwkv_kernel, rows=rows, chunk=chunk),
        out_shape=jax.ShapeDtypeStruct((batch * seq, RW_WIDTH), BF16),
        grid=(batch, nb),
        in_specs=[
            pl.BlockSpec((rows, w4), lambda b, c: (b * nb + c, COL_RW // w4)),
            pl.BlockSpec((rows, RW_LORA), lambda b, c: (b * nb + c, COL_LORA // RW_LORA)),
            full((1, w4)), full((1, RW_LORA)),
            full((1, RW_WIDTH)), full((RW_LORA, RW_WIDTH)),
            full((1, RW_WIDTH)), full((RW_LORA, RW_WIDTH)),
            full((1, RW_WIDTH)), full((1, RW_WIDTH)), full((1, RW_WIDTH)),
            full((1, RW_WIDTH)), full((1, RW_WIDTH)),
            full((chunk, chunk)), full((RW_WIDTH, RW_WIDTH)),
        ],
        out_specs=pl.BlockSpec((rows, RW_WIDTH), lambda b, c: (b * nb + c, 0)),
        scratch_shapes=[
            pltpu.VMEM((rows + 8, w4), F32),
            pltpu.VMEM((rows + 8, RW_LORA), F32),
            pltpu.VMEM((RW_WIDTH // LANES, LANES, LANES), F32),
            pltpu.VMEM((rows, RW_WIDTH), F32),
        ],
        compiler_params=pltpu.CompilerParams(
            dimension_semantics=("parallel", "arbitrary"), vmem_limit_bytes=VMEM_LIMIT),
        name="rwkv7",
    )(proj, proj, vec(mu[:w4]), vec(mu[w4:]), vec(w0), wup_pad, vec(a0), aup_pad, vec(k_k), vec(k_a),
      vec(r_k), vec(ln_g), vec(ln_b), tri, hsum)


def kernel(x, norm_g, w_in, conv_w, conv_b, dt_bias, a_log, d_skip, ssd_norm_g, rw_mu, rw_w0, rw_w_up, rw_a0,
           rw_a_up, rw_k_k, rw_k_a, rw_r_k, rw_ln_g, rw_ln_b, w_out_sb, w_out_ssd, w_out_rw, w_o, final_g):
    batch, seq, d = x.shape
    depth = w_in.shape[0]
    x2 = x.reshape(batch * seq, d)
    for i in range(depth):
        proj = _inproj(x2, norm_g[i], _pack_in_cols(w_in[i]).astype(BF16))
        y_sb = _sb_attention(proj, batch, seq)
        y_ssd = _ssd(proj, batch, seq, conv_w[i], conv_b[i], dt_bias[i], a_log[i], d_skip[i], ssd_norm_g[i])
        y_rw = _rwkv(proj, batch, seq, rw_mu[i], rw_w0[i], rw_w_up[i], rw_a0[i], rw_a_up[i], rw_k_k[i],
                     rw_k_a[i], rw_r_k[i].reshape(-1), rw_ln_g[i], rw_ln_b[i])
        x2 = _merge(x2, y_sb, y_ssd, y_rw, proj, w_out_sb[i].astype(BF16), w_out_ssd[i].astype(BF16),
                    w_out_rw[i].astype(BF16), w_o[i].astype(BF16), final_g, final_norm=(i == depth - 1))
    return x2.reshape(batch, seq, d)
```

```python
import functools

import jax
import jax.numpy as jnp
from jax import lax
from jax.experimental import pallas as pl
from jax.experimental.pallas import tpu as pltpu

F32 = jnp.float32
BF16 = jnp.bfloat16

D_MODEL = 1024
HEAD = 64
LANES = 128
SB_WIDTH = 512
SSD_WIDTH = 1024
SSD_HEADS = 16
SSD_GROUPS = 2
SSD_STATE = 64
SSD_CONV = 4
RW_WIDTH = 512
RW_LORA = 128
RMS_EPS = 1e-6
GN_EPS = 64e-5

COL_SB = 0
COL_RW = 2048
COL_Z = 4096
COL_X = 5120
COL_GATES = 6144
N_WIDE = 9216
COL_BC = 0
COL_LORA = 256
COL_DT = 384
N_NARROW = 512

VMEM_LIMIT = 48 * 1024 * 1024


def _dot(a, b):
    return jnp.dot(a, b, preferred_element_type=F32)


def _dot_nt(a, b):
    return lax.dot_general(a, b, (((1,), (1,)), ((), ())), preferred_element_type=F32)


def _dot_tn(a, b):
    return lax.dot_general(a, b, (((0,), (0,)), ((), ())), preferred_element_type=F32)


def _split(x, parts):
    out = []
    for _ in range(parts - 1):
        p = x.astype(BF16)
        out.append(p)
        x = x - p.astype(F32)
    out.append(x.astype(BF16))
    return out


def _dot_exact_rhs(a, b_bf16, parts=3):
    acc = None
    for p in _split(a, parts):
        t = _dot(p, b_bf16)
        acc = t if acc is None else acc + t
    return acc


def _dot_exact_lhs(a_bf16, b, parts=3):
    acc = None
    for p in _split(b, parts):
        t = _dot(a_bf16, p)
        acc = t if acc is None else acc + t
    return acc


def _softplus(x):
    return jnp.maximum(x, 0.0) + jnp.log1p(jnp.exp(-jnp.abs(x)))


def _sigmoid(x):
    return 1.0 / (1.0 + jnp.exp(-x))


def _silu(x):
    return x * _sigmoid(x)


def _rms_rows(x, g):
    ms = jnp.mean(x * x, axis=-1, keepdims=True)
    return x * lax.rsqrt(ms + RMS_EPS) * g


def _pack_wide(w):
    return jnp.concatenate([
        w[..., 0:2048],
        w[..., 4368:6416],
        w[..., 2048:3072],
        w[..., 3072:4096],
        w[..., 6544:9616],
    ], axis=-1)


def _pack_narrow(w):
    pad = jnp.zeros(w.shape[:-1] + (N_NARROW - COL_DT - SSD_HEADS,), w.dtype)
    return jnp.concatenate([
        w[..., 4096:4352],
        w[..., 6416:6544],
        w[..., 4352:4368],
        pad], axis=-1)


def _inproj_kernel(x_ref, g_ref, w_ref, o_ref, h_ref, *, rows):
    @pl.when(pl.program_id(1) == 0)
    def _():
        def body(r, c):
            sl = pl.ds(pl.multiple_of(r * rows, rows), rows)
            h_ref[sl, :] = _rms_rows(x_ref[sl, :], g_ref[...]).astype(BF16)
            return c
        lax.fori_loop(0, x_ref.shape[0] // rows, body, 0)

    o_ref[...] = _dot(h_ref[...], w_ref[...]).astype(o_ref.dtype)


def _inproj(x2, g, w_pack, out_dtype, *, tm=1024, tn=1024):
    t, d = x2.shape
    n = w_pack.shape[1]
    tn = min(tn, n)
    return pl.pallas_call(
        functools.partial(_inproj_kernel, rows=min(tm, 256)),
        out_shape=jax.ShapeDtypeStruct((t, n), out_dtype),
        grid=(t // tm, n // tn),
        in_specs=[
            pl.BlockSpec((tm, d), lambda i, j: (i, 0)),
            pl.BlockSpec((1, d), lambda i, j: (0, 0)),
            pl.BlockSpec((d, tn), lambda i, j: (0, j)),
        ],
        out_specs=pl.BlockSpec((tm, tn), lambda i, j: (i, j)),
        scratch_shapes=[pltpu.VMEM((tm, d), BF16)],
        compiler_params=pltpu.CompilerParams(
            dimension_semantics=("parallel", "arbitrary"), vmem_limit_bytes=VMEM_LIMIT),
        name="inproj",
    )(x2, g.reshape(1, d), w_pack)


SB_MASKED = -1e30
LOG2_E = 1.4426950408889634


def _sb_kernel(q_ref, k_ref, v_ref, gate_ref, suffix_ref, o_ref,
               drop_ref, z_ref, before_ref, p_ref, acc_ref, run_ref, *, tq, tk):
    qi = pl.program_id(2)
    per_q = tq // tk
    n = (qi + 1) * per_q
    lane = lax.broadcasted_iota(jnp.int32, (tq, LANES), 1)
    first = lane < HEAD
    q = q_ref[...] * jnp.asarray(HEAD ** -0.5, BF16)
    none = jnp.zeros_like(q)
    q_stack = jnp.concatenate([jnp.where(first, q, none), jnp.where(first, none, q)], axis=0)
    suffix = suffix_ref[...]
    acc_ref[...] = jnp.zeros_like(acc_ref)
    run_ref[...] = jnp.zeros_like(run_ref)

    def keys(t):
        return pl.ds(pl.multiple_of((n - 1 - t) * tk, tk), tk)

    def stage_a(t, slot, masked):
        z = _dot_nt(q_stack, k_ref[keys(t), :]) * LOG2_E
        drop = jnp.maximum(z, 0.0) + jnp.log2(1.0 + jnp.exp2(-jnp.abs(z)))
        if masked:
            row = lax.broadcasted_iota(jnp.int32, z.shape, 0)
            col = lax.broadcasted_iota(jnp.int32, z.shape, 1)
            q_pos = jnp.where(row >= tq, row - tq, row) + qi * tq
            strict = col + (n - 1 - t) * tk < q_pos
            drop = jnp.where(strict, drop, 0.0)
            z = jnp.where(strict, z, SB_MASKED)
        drop_ref[slot] = drop.astype(BF16)
        z_ref[slot] = z
        run = run_ref[...]
        before_ref[slot] = run
        run_ref[...] = run + jnp.sum(drop, axis=-1, keepdims=True)

    def stage_b(slot):
        inside = _dot(drop_ref[slot], suffix)
        before = before_ref[slot]
        before = jnp.concatenate([before] * (tk // LANES), axis=1)
        p_ref[slot] = jnp.exp2(z_ref[slot] - inside - before).astype(BF16)

    def stage_c(t, slot):
        acc_ref[...] += _dot(p_ref[slot], v_ref[keys(t), :])

    stage_a(0, 0, True)
    stage_a(1, 1, True)
    stage_b(0)

    def body(i, c):
        t = 2 * i
        stage_c(t - 2, 0)
        stage_b(1)
        stage_a(t, 0, False)
        stage_c(t - 1, 1)
        stage_b(0)
        stage_a(t + 1, 1, False)
        return c
    lax.fori_loop(1, n // 2, body, 0)
    stage_c(n - 2, 0)
    stage_b(1)
    stage_c(n - 1, 1)
    acc = acc_ref[...]
    y = jnp.where(first, acc[:tq], acc[tq:])
    o_ref[...] = (y * _silu(gate_ref[...].astype(F32))).astype(BF16)


def _sb_attention(proj, batch, seq, *, tq=512, tk=256):
    assert tq == 2 * tk, "the pipeline prologue assumes two masked key tiles per query tile"
    nq = seq // tq
    pairs = SB_WIDTH // LANES
    c0 = COL_SB // LANES
    idx = jnp.arange(tk)
    suffix = (idx[:, None] >= idx[None, :]).astype(BF16)
    rows = 2 * tq
    return pl.pallas_call(
        functools.partial(_sb_kernel, tq=tq, tk=tk),
        out_shape=jax.ShapeDtypeStruct((batch * seq, SB_WIDTH), BF16),
        grid=(batch, pairs, nq),
        in_specs=[
            pl.BlockSpec((tq, LANES), lambda b, p, i: (b * nq + i, c0 + p)),
            pl.BlockSpec((seq, LANES), lambda b, p, i: (b, c0 + pairs + p)),
            pl.BlockSpec((seq, LANES), lambda b, p, i: (b, c0 + 2 * pairs + p)),
            pl.BlockSpec((tq, LANES), lambda b, p, i: (b * nq + i, c0 + 3 * pairs + p)),
            pl.BlockSpec((tk, tk), lambda b, p, i: (0, 0)),
        ],
        out_specs=pl.BlockSpec((tq, LANES), lambda b, p, i: (b * nq + i, p)),
        scratch_shapes=[
            pltpu.VMEM((2, rows, tk), BF16),
            pltpu.VMEM((2, rows, tk), F32),
            pltpu.VMEM((2, rows, LANES), F32),
            pltpu.VMEM((2, rows, tk), BF16),
            pltpu.VMEM((rows, LANES), F32),
            pltpu.VMEM((rows, LANES), F32),
        ],
        compiler_params=pltpu.CompilerParams(
            dimension_semantics=("parallel", "parallel", "arbitrary"), vmem_limit_bytes=VMEM_LIMIT),
        name="sb_attention",
    )(proj, proj, proj, proj, suffix)


def _ssd_kernel(z_ref, x_ref, bc_ref, dt_ref, cwx_ref, cwbc_ref, cbx_ref, cbbc_ref, dtb_ref, ah_ref,
                dskip_ref, ng_ref, tri_ref, e1_ref, e2_ref, o_ref, xpad_ref, bcpad_ref, state_ref, *, chunk):
    ci = pl.program_id(1)
    halo = 8

    @pl.when(ci == 0)
    def _():
        state_ref[...] = jnp.zeros_like(state_ref)
        xpad_ref[0:halo, :] = jnp.zeros((halo, xpad_ref.shape[1]), F32)
        bcpad_ref[0:halo, :] = jnp.zeros((halo, bcpad_ref.shape[1]), F32)

    xpad_ref[halo:halo + chunk, :] = x_ref[...].astype(F32)
    bcpad_ref[halo:halo + chunk, :] = bc_ref[...]

    def conv(pad_ref, w_ref, b_ref):
        acc = b_ref[...]
        for k in range(SSD_CONV):
            off = halo - (SSD_CONV - 1) + k
            acc = acc + w_ref[k:k + 1, :] * pad_ref[off:off + chunk, :]
        return _silu(acc)

    xs = conv(xpad_ref, cwx_ref, cbx_ref)
    bc = conv(bcpad_ref, cwbc_ref, cbbc_ref)
    xpad_ref[0:halo, :] = xpad_ref[chunk:chunk + halo, :]
    bcpad_ref[0:halo, :] = bcpad_ref[chunk:chunk + halo, :]
    bm = bc[:, :LANES]
    cm = bc[:, LANES:]

    dt = _softplus(dt_ref[...] + dtb_ref[...])
    da = dt * ah_ref[...]
    acs = _dot_exact_lhs(tri_ref[...], da)
    e1 = e1_ref[...]
    dt_e = _dot_exact_rhs(dt, e1)
    acs_e = _dot_exact_rhs(acs, e1)
    acs_e2 = _dot_exact_rhs(acs, e2_ref[...])
    acs_t = acs.T

    x_dt = xs * dt_e
    last = acs_e[chunk - 1:chunk, :]
    xs_dec = (x_dt * jnp.exp(last - acs_e)).astype(BF16)

    lane = lax.broadcasted_iota(jnp.int32, (chunk, LANES), 1)
    first = lane < HEAD
    row = lax.broadcasted_iota(jnp.int32, (chunk, chunk), 0)
    col = lax.broadcasted_iota(jnp.int32, (chunk, chunk), 1)
    causal = col <= row
    bm16 = bm.astype(BF16)
    cb = [_dot_nt(jnp.where(first, cm, 0.0).astype(BF16), bm16),
          _dot_nt(jnp.where(first, 0.0, cm).astype(BF16), bm16)]
    x_dt16 = x_dt.astype(BF16)

    heads_per_group = SSD_HEADS // SSD_GROUPS
    pieces = []
    for pair in range(SSD_HEADS // 2):
        xp = x_dt16[:, pair * LANES:(pair + 1) * LANES]
        yd = []
        for h in (2 * pair, 2 * pair + 1):
            seg = acs_e2[:, h * LANES:(h + 1) * LANES] - acs_t[h:h + 1, :]
            m = jnp.where(causal, jnp.exp(seg), 0.0) * cb[h // heads_per_group]
            yd.append(_dot(m.astype(BF16), xp))
        pieces.append(jnp.where(first, yd[0], yd[1]))
    y_diag = jnp.concatenate(pieces, axis=1)

    state = state_ref[...]
    y_off = _dot(cm.astype(BF16), state.astype(BF16)) * jnp.exp(acs_e)
    srow = lax.broadcasted_iota(jnp.int32, state.shape, 0)
    slane = lax.broadcasted_iota(jnp.int32, state.shape, 1)
    same_group = (srow < SSD_STATE) == (slane < SSD_WIDTH // SSD_GROUPS)
    state_ref[...] = state * jnp.exp(last) + jnp.where(same_group, _dot_tn(bm16, xs_dec), 0.0)

    y = y_diag + y_off + xs * dskip_ref[...]
    y = y * _silu(z_ref[...].astype(F32))
    o_ref[...] = _rms_rows(y, ng_ref[...]).astype(BF16)


def _ssd(wide, narrow, batch, seq, conv_w, conv_b, dt_bias, a_log, d_skip, norm_g, *, chunk=128):
    nc = seq // chunk
    nbc = 2 * SSD_GROUPS * SSD_STATE
    pad16 = lambda v: jnp.pad(v, (0, LANES - SSD_HEADS)).reshape(1, LANES)
    idx = jnp.arange(chunk)
    tri = (idx[None, :] <= idx[:, None]).astype(BF16)
    hidx = jnp.arange(LANES)[:, None]
    e1 = (jnp.arange(SSD_WIDTH)[None, :] // HEAD == hidx).astype(BF16)
    e2 = (jnp.arange(SSD_HEADS * LANES)[None, :] // LANES == hidx).astype(BF16)
    full = lambda shape: pl.BlockSpec(shape, lambda b, c: (0, 0))
    row_blk = lambda width, colblk: pl.BlockSpec((chunk, width), lambda b, c: (b * nc + c, colblk))
    return pl.pallas_call(
        functools.partial(_ssd_kernel, chunk=chunk),
        out_shape=jax.ShapeDtypeStruct((batch * seq, SSD_WIDTH), BF16),
        grid=(batch, nc),
        in_specs=[
            row_blk(SSD_WIDTH, COL_Z // SSD_WIDTH),
            row_blk(SSD_WIDTH, COL_X // SSD_WIDTH),
            row_blk(nbc, COL_BC // nbc),
            row_blk(LANES, COL_DT // LANES),
            full((SSD_CONV, SSD_WIDTH)), full((SSD_CONV, nbc)),
            full((1, SSD_WIDTH)), full((1, nbc)),
            full((1, LANES)), full((1, LANES)),
            full((1, SSD_WIDTH)), full((1, SSD_WIDTH)),
            full((chunk, chunk)), full((LANES, SSD_WIDTH)), full((LANES, SSD_HEADS * LANES)),
        ],
        out_specs=pl.BlockSpec((chunk, SSD_WIDTH), lambda b, c: (b * nc + c, 0)),
        scratch_shapes=[
            pltpu.VMEM((chunk + 8, SSD_WIDTH), F32),
            pltpu.VMEM((chunk + 8, nbc), F32),
            pltpu.VMEM((LANES, SSD_WIDTH), F32),
        ],
        compiler_params=pltpu.CompilerParams(
            dimension_semantics=("parallel", "arbitrary"), vmem_limit_bytes=VMEM_LIMIT),
        name="ssd",
    )(wide, wide, narrow, narrow,
      conv_w[:, :SSD_WIDTH], conv_w[:, SSD_WIDTH:],
      conv_b[:SSD_WIDTH].reshape(1, -1), conv_b[SSD_WIDTH:].reshape(1, -1),
      pad16(dt_bias), pad16(-jnp.exp(a_log)),
      jnp.repeat(d_skip, HEAD).reshape(1, -1), norm_g.reshape(1, -1),
      tri, e1, e2)


def _merge_kernel(x_ref, ysb_ref, yssd_ref, yrw_ref, gates_ref, wsb_ref, wssd_ref, wrw_ref, wo_ref, fg_ref,
                  o_ref, *, final_norm):
    g = _sigmoid(gates_ref[...].astype(F32))
    merged = (g[:, :D_MODEL] * _dot(ysb_ref[...], wsb_ref[...])
              + g[:, D_MODEL:2 * D_MODEL] * _dot(yssd_ref[...], wssd_ref[...])
              + g[:, 2 * D_MODEL:] * _dot(yrw_ref[...], wrw_ref[...]))
    out = x_ref[...] + _dot(merged.astype(BF16), wo_ref[...])
    if final_norm:
        out = _rms_rows(out, fg_ref[...])
    o_ref[...] = out


def _merge(x2, y_sb, y_ssd, y_rw, proj, w_sb, w_ssd, w_rw, w_o, final_g, *, final_norm, tm=512):
    t, d = x2.shape
    rows = lambda width, colblk=0: pl.BlockSpec((tm, width), lambda i: (i, colblk))
    full = lambda shape: pl.BlockSpec(shape, lambda i: (0, 0))
    return pl.pallas_call(
        functools.partial(_merge_kernel, final_norm=final_norm),
        out_shape=jax.ShapeDtypeStruct((t, d), F32),
        grid=(t // tm,),
        in_specs=[
            rows(d), rows(SB_WIDTH), rows(SSD_WIDTH), rows(RW_WIDTH),
            rows(3 * d, COL_GATES // (3 * d)),
            full((SB_WIDTH, d)), full((SSD_WIDTH, d)), full((RW_WIDTH, d)), full((d, d)), full((1, d)),
        ],
        out_specs=rows(d),
        compiler_params=pltpu.CompilerParams(
            dimension_semantics=("parallel",), vmem_limit_bytes=VMEM_LIMIT),
        name="merge",
    )(x2, y_sb, y_ssd, y_rw, proj, w_sb, w_ssd, w_rw, w_o, final_g.reshape(1, d))


def _dot2(a, b, dims=None):
    a_hi, a_lo = _split(a, 2)
    b_hi, b_lo = _split(b, 2)
    f = _dot if dims is None else dims
    return f(a_hi, b_hi) + f(a_hi, b_lo) + f(a_lo, b_hi)


def _rwkv_kernel(rkvg_ref, lora_ref, mu_ref, mul_ref, w0_ref, wup_ref, a0_ref, aup_ref, kk_ref, ka_ref, rk_ref,
                 lng_ref, lnb_ref, tri_ref, hsum_ref, o_ref, pad_ref, padl_ref, state_ref, y_ref, *, rows, chunk):
    halo = 8

    @pl.when(pl.program_id(1) == 0)
    def _():
        state_ref[...] = jnp.zeros_like(state_ref)
        pad_ref[0:halo, :] = jnp.zeros((halo, pad_ref.shape[1]), F32)
        padl_ref[0:halo, :] = jnp.zeros((halo, padl_ref.shape[1]), F32)

    def shift_mix(ref, pad, mu):
        cur = ref[...].astype(F32)
        pad[halo:halo + rows, :] = cur
        prev = pad[halo - 1:halo - 1 + rows, :]
        pad[0:halo, :] = pad[rows:rows + halo, :]
        return cur + (prev - cur) * mu[...]

    mixed = shift_mix(rkvg_ref, pad_ref, mu_ref)
    lora = shift_mix(lora_ref, padl_ref, mul_ref)
    w = RW_WIDTH
    r, k, v, gate = (mixed[:, i * w:(i + 1) * w] for i in range(4))
    hsum = hsum_ref[...]

    w_raw = -_softplus(-(w0_ref[...] + _dot2(jnp.tanh(lora), wup_ref[...]))) - 0.5
    log_w = -jnp.exp(w_raw)
    a = _sigmoid(a0_ref[...] + _dot2(lora, aup_ref[...]))
    kk = k * kk_ref[...]
    kk = kk / jnp.maximum(jnp.sqrt(_dot_exact_rhs(kk * kk, hsum, 2)), 1e-12)
    k = k * (1.0 + (a - 1.0) * ka_ref[...])
    bonus = _dot_exact_rhs(r * k * rk_ref[...], hsum, 2) * v
    b = kk * a

    lane = lax.broadcasted_iota(jnp.int32, (chunk, LANES), 1)
    first = lane < HEAD
    n = 2 * chunk
    srow = lax.broadcasted_iota(jnp.int32, (n, n), 0)
    scol = lax.broadcasted_iota(jnp.int32, (n, n), 1)
    strict = scol < srow
    causal = scol <= srow
    eye = (srow == scol).astype(F32)
    drow = lax.broadcasted_iota(jnp.int32, (LANES, LANES), 0)
    dcol = lax.broadcasted_iota(jnp.int32, (LANES, LANES), 1)
    diag = drow == dcol

    nc = rows // chunk
    pairs = RW_WIDTH // LANES
    cl = _dot_exact_lhs(tri_ref[...], log_w)
    cl3 = cl.reshape(nc, chunk, RW_WIDTH)
    last3 = cl3[:, chunk - 1:chunk, :]
    c_tail = jnp.exp(last3 - cl3).reshape(rows, RW_WIDTH)
    c_inv = jnp.exp(-cl)
    c_last = jnp.exp(last3)

    def stacks(x):
        x = x.astype(BF16)
        none = jnp.zeros((chunk, LANES), BF16)
        out = []
        for c in range(nc):
            for p in range(pairs):
                blk = x[c * chunk:(c + 1) * chunk, p * LANES:(p + 1) * LANES]
                out.append(jnp.concatenate([jnp.where(first, blk, none), jnp.where(first, none, blk)], axis=0))
        return jnp.stack(out)

    def bmm(x, y, lhs_dim=2, rhs_dim=1):
        dims = (((lhs_dim,), (rhs_dim,)), ((0,), (0,)))
        return lax.dot_general(x.astype(BF16), y.astype(BF16), dims, preferred_element_type=F32)

    a_s = stacks(-kk * jnp.exp(cl - log_w))
    r_s = stacks(r * jnp.exp(cl))
    b_s = stacks(b * c_inv)
    k_s = stacks(k * c_inv)
    bh_s = stacks(b * c_tail)
    kh_s = stacks(k * c_tail)
    v_s = stacks(v)
    a_ab = jnp.where(strict, bmm(a_s, b_s, 2, 2), 0.0)
    a_ak = jnp.where(strict, bmm(a_s, k_s, 2, 2), 0.0)
    a_rb = jnp.where(causal, bmm(r_s, b_s, 2, 2), 0.0).astype(BF16)
    a_rk = jnp.where(causal, bmm(r_s, k_s, 2, 2), 0.0)
    x = a_ab
    t = eye + x
    m = 2
    while m < chunk:
        x = bmm(x, x)
        t = t + bmm(t, x)
        m *= 2
    t = t.astype(BF16)
    p1 = bmm(t, a_s)
    q1 = bmm(t, bmm(a_ak, v_s))
    p2 = (r_s.astype(F32) + bmm(a_rb, p1)).astype(BF16)
    q2 = bmm(a_rb, q1) + bmm(a_rk, v_s)
    g_low = bmm(bh_s, p1, 1, 1)
    h = bmm(bh_s, q1, 1, 1) + bmm(kh_s, v_s, 1, 1)

    for c in range(nc):
        rs = slice(c * chunk, (c + 1) * chunk)
        for p in range(pairs):
            ls = slice(p * LANES, (p + 1) * LANES)
            i = c * pairs + p
            st = state_ref[p]
            st16 = st.astype(BF16)
            ys = _dot(p2[i], st16) + q2[i]
            y_ref[rs, ls] = ys[:chunk] + ys[chunk:]
            g = jnp.where(diag, c_last[c, :, ls], 0.0) + g_low[i]
            state_ref[p] = _dot(g.astype(BF16), st16) + h[i]

    y = y_ref[...]
    inv = 1.0 / HEAD
    mean = _dot_exact_rhs(y, hsum, 2) * inv
    d = y - mean
    var = _dot_exact_rhs(d * d, hsum, 2) * inv
    y = d * lax.rsqrt(var + GN_EPS) * lng_ref[...] + lnb_ref[...] + bonus
    o_ref[...] = (y * _silu(gate)).astype(BF16)


def _rwkv(wide, narrow, batch, seq, mu, w0, w_up, a0, a_up, k_k, k_a, r_k, ln_g, ln_b, *, rows=256, chunk=64):
    nb = seq // rows
    w4 = 4 * RW_WIDTH
    half = RW_LORA // 2
    idx = jnp.arange(rows)
    tri = ((idx[None, :] <= idx[:, None]) & (idx[None, :] // chunk == idx[:, None] // chunk)).astype(BF16)
    lanes = jnp.arange(RW_WIDTH)
    hsum = (lanes[:, None] // HEAD == lanes[None, :] // HEAD).astype(BF16)
    wup_pad = jnp.concatenate([w_up, jnp.zeros((half, RW_WIDTH), F32)], axis=0)
    aup_pad = jnp.concatenate([jnp.zeros((half, RW_WIDTH), F32), a_up], axis=0)
    vec = lambda a: a.reshape(1, -1)
    full = lambda shape: pl.BlockSpec(shape, lambda b, c: (0, 0))
    return pl.pallas_call(
        functools.partial(_rwkv_kernel, rows=rows, chunk=chunk),
        out_shape=jax.ShapeDtypeStruct((batch * seq, RW_WIDTH), BF16),
        grid=(batch, nb),
        in_specs=[
            pl.BlockSpec((rows, w4), lambda b, c: (b * nb + c, COL_RW // w4)),
            pl.BlockSpec((rows, RW_LORA), lambda b, c: (b * nb + c, COL_LORA // RW_LORA)),
            full((1, w4)), full((1, RW_LORA)),
            full((1, RW_WIDTH)), full((RW_LORA, RW_WIDTH)),
            full((1, RW_WIDTH)), full((RW_LORA, RW_WIDTH)),
            full((1, RW_WIDTH)), full((1, RW_WIDTH)), full((1, RW_WIDTH)),
            full((1, RW_WIDTH)), full((1, RW_WIDTH)),
            full((rows, rows)), full((RW_WIDTH, RW_WIDTH)),
        ],
        out_specs=pl.BlockSpec((rows, RW_WIDTH), lambda b, c: (b * nb + c, 0)),
        scratch_shapes=[
            pltpu.VMEM((rows + 8, w4), F32),
            pltpu.VMEM((rows + 8, RW_LORA), F32),
            pltpu.VMEM((RW_WIDTH // LANES, LANES, LANES), F32),
            pltpu.VMEM((rows, RW_WIDTH), F32),
        ],
        compiler_params=pltpu.CompilerParams(
            dimension_semantics=("parallel", "arbitrary"), vmem_limit_bytes=VMEM_LIMIT),
        name="rwkv7",
    )(wide, narrow, vec(mu[:w4]), vec(mu[w4:]), vec(w0), wup_pad, vec(a0), aup_pad, vec(k_k), vec(k_a),
      vec(r_k), vec(ln_g), vec(ln_b), tri, hsum)


def kernel(x, norm_g, w_in, conv_w, conv_b, dt_bias, a_log, d_skip, ssd_norm_g, rw_mu, rw_w0, rw_w_up, rw_a0,
           rw_a_up, rw_k_k, rw_k_a, rw_r_k, rw_ln_g, rw_ln_b, w_out_sb, w_out_ssd, w_out_rw, w_o, final_g):
    batch, seq, d = x.shape
    depth = w_in.shape[0]
    x2 = x.reshape(batch * seq, d)
    for i in range(depth):
        wide = _inproj(x2, norm_g[i], _pack_wide(w_in[i]).astype(BF16), BF16)
        narrow = _inproj(x2, norm_g[i], _pack_narrow(w_in[i]).astype(BF16), F32)
        y_sb = _sb_attention(wide, batch, seq)
        y_ssd = _ssd(wide, narrow, batch, seq, conv_w[i], conv_b[i], dt_bias[i], a_log[i], d_skip[i],
                     ssd_norm_g[i])
        y_rw = _rwkv(wide, narrow, batch, seq, rw_mu[i], rw_w0[i], rw_w_up[i], rw_a0[i], rw_a_up[i], rw_k_k[i],
                     rw_k_a[i], rw_r_k[i].reshape(-1), rw_ln_g[i], rw_ln_b[i])
        x2 = _merge(x2, y_sb, y_ssd, y_rw, wide, w_out_sb[i].astype(BF16), w_out_ssd[i].astype(BF16),
                    w_out_rw[i].astype(BF16), w_o[i].astype(BF16), final_g, final_norm=(i == depth - 1))
    return x2.reshape(batch, seq, d)
```

```python
import functools

import jax
import jax.numpy as jnp
from jax import lax
from jax.experimental import pallas as pl
from jax.experimental.pallas import tpu as pltpu

F32 = jnp.float32
BF16 = jnp.bfloat16

D_MODEL = 1024
HEAD = 64
LANES = 128
SB_WIDTH = 512
SSD_WIDTH = 1024
SSD_HEADS = 16
SSD_GROUPS = 2
SSD_STATE = 64
SSD_CONV = 4
RW_WIDTH = 512
RW_LORA = 128
RMS_EPS = 1e-6
GN_EPS = 64e-5

COL_SB = 0
COL_RW = 2048
COL_Z = 4096
COL_X = 5120
COL_GATES = 6144
N_WIDE = 9216
COL_BC = 0
COL_LORA = 256
COL_DT = 384
N_NARROW = 512

VMEM_LIMIT = 48 * 1024 * 1024


def _dot(a, b):
    return jnp.dot(a, b, preferred_element_type=F32)


def _dot_nt(a, b):
    return lax.dot_general(a, b, (((1,), (1,)), ((), ())), preferred_element_type=F32)


def _dot_tn(a, b):
    return lax.dot_general(a, b, (((0,), (0,)), ((), ())), preferred_element_type=F32)


def _split(x, parts):
    out = []
    for _ in range(parts - 1):
        p = x.astype(BF16)
        out.append(p)
        x = x - p.astype(F32)
    out.append(x.astype(BF16))
    return out


def _dot_exact_rhs(a, b_bf16, parts=3):
    acc = None
    for p in _split(a, parts):
        t = _dot(p, b_bf16)
        acc = t if acc is None else acc + t
    return acc


def _dot_exact_lhs(a_bf16, b, parts=3):
    acc = None
    for p in _split(b, parts):
        t = _dot(a_bf16, p)
        acc = t if acc is None else acc + t
    return acc


def _softplus(x):
    return jnp.maximum(x, 0.0) + jnp.log1p(jnp.exp(-jnp.abs(x)))


def _sigmoid(x):
    return 1.0 / (1.0 + jnp.exp(-x))


def _silu(x):
    return x * _sigmoid(x)


def _rms_rows(x, g):
    ms = jnp.mean(x * x, axis=-1, keepdims=True)
    return x * lax.rsqrt(ms + RMS_EPS) * g


def _pack_wide(w):
    return jnp.concatenate([
        w[..., 0:2048],
        w[..., 4368:6416],
        w[..., 2048:3072],
        w[..., 3072:4096],
        w[..., 6544:9616],
    ], axis=-1)


def _pack_narrow(w):
    pad = jnp.zeros(w.shape[:-1] + (N_NARROW - COL_DT - SSD_HEADS,), w.dtype)
    return jnp.concatenate([
        w[..., 4096:4352],
        w[..., 6416:6544],
        w[..., 4352:4368],
        pad], axis=-1)


def _inproj_kernel(x_ref, g_ref, w_ref, wn_ref, o_ref, on_ref, h_ref, *, rows):
    @pl.when(pl.program_id(1) == 0)
    def _():
        def body(r, c):
            sl = pl.ds(pl.multiple_of(r * rows, rows), rows)
            h_ref[sl, :] = _rms_rows(x_ref[sl, :], g_ref[...]).astype(BF16)
            return c
        lax.fori_loop(0, x_ref.shape[0] // rows, body, 0)
        on_ref[...] = _dot(h_ref[...], wn_ref[...])

    o_ref[...] = _dot(h_ref[...], w_ref[...]).astype(o_ref.dtype)


def _inproj(x2, g, w_wide, w_narrow, *, tm=1024, tn=1024):
    t, d = x2.shape
    n = w_wide.shape[1]
    nn = w_narrow.shape[1]
    return pl.pallas_call(
        functools.partial(_inproj_kernel, rows=min(tm, 256)),
        out_shape=(jax.ShapeDtypeStruct((t, n), BF16), jax.ShapeDtypeStruct((t, nn), F32)),
        grid=(t // tm, n // tn),
        in_specs=[
            pl.BlockSpec((tm, d), lambda i, j: (i, 0)),
            pl.BlockSpec((1, d), lambda i, j: (0, 0)),
            pl.BlockSpec((d, tn), lambda i, j: (0, j)),
            pl.BlockSpec((d, nn), lambda i, j: (0, 0)),
        ],
        out_specs=(pl.BlockSpec((tm, tn), lambda i, j: (i, j)), pl.BlockSpec((tm, nn), lambda i, j: (i, 0))),
        scratch_shapes=[pltpu.VMEM((tm, d), BF16)],
        compiler_params=pltpu.CompilerParams(
            dimension_semantics=("parallel", "arbitrary"), vmem_limit_bytes=VMEM_LIMIT),
        name="inproj",
    )(x2, g.reshape(1, d), w_wide, w_narrow)


SB_MASKED = -1e30
SB_UNDERFLOW_BITS = 160.0
LOG2_E = 1.4426950408889634


def _sb_kernel(q_ref, k_ref, v_ref, gate_ref, suffix_ref, o_ref, acc_ref, run_ref, *, tile):
    qi = pl.program_id(2)
    rows = 2 * tile
    lane = lax.broadcasted_iota(jnp.int32, (tile, LANES), 1)
    first = lane < HEAD
    suffix = suffix_ref[...]
    row = lax.broadcasted_iota(jnp.int32, (rows, tile), 0)
    col = lax.broadcasted_iota(jnp.int32, (rows, tile), 1)
    strict = col < jnp.where(row >= tile, row - tile, row)

    def q_stack(sub):
        q = q_ref[sub * tile:(sub + 1) * tile, :] * jnp.asarray(HEAD ** -0.5, BF16)
        none = jnp.zeros_like(q)
        return jnp.concatenate([jnp.where(first, q, none), jnp.where(first, none, q)], axis=0)

    def keys(j):
        return pl.ds(pl.multiple_of(j * tile, tile), tile)

    def stage_a(q, j, masked):
        z = _dot_nt(q, k_ref[keys(j), :]) * LOG2_E
        drop = jnp.maximum(z, 0.0) + jnp.log2(1.0 + jnp.exp2(-jnp.abs(z)))
        if masked:
            drop = jnp.where(strict, drop, 0.0)
            z = jnp.where(strict, z, SB_MASKED)
        total = jnp.broadcast_to(jnp.sum(drop, axis=-1, keepdims=True), (rows, LANES))
        return z, drop.astype(BF16), total

    def stage_b(z, drop, before):
        x = z - _dot(drop, suffix)
        if before is not None:
            x = x - jnp.concatenate([before] * (tile // LANES), axis=1)
        return jnp.exp2(x).astype(BF16)

    def stage_c(p, j):
        return _dot(p, v_ref[keys(j), :])

    def near(with_previous):
        q_a, q_b = q_stack(0), q_stack(1)
        j = 2 * qi
        units = [(q_a, j, True), (q_b, j + 1, True), (q_b, j, False)]
        if with_previous:
            units.append((q_a, j - 1, False))
        a = [stage_a(*u) for u in units]
        before = [None, None, a[1][2], a[0][2]]
        p = [stage_b(a[i][0], a[i][1], before[i]) for i in range(len(units))]
        c = [stage_c(p[i], units[i][1]) for i in range(len(units))]
        acc_ref[rows:, :] = c[1] + c[2]
        run_ref[rows:, :] = a[1][2] + a[2][2]
        if with_previous:
            acc_ref[:rows, :] = c[0] + c[3]
            run_ref[:rows, :] = a[0][2] + a[3][2]
        else:
            acc_ref[:rows, :] = c[0]
            run_ref[:rows, :] = a[0][2]

    @pl.when(qi > 0)
    def _():
        near(True)

    @pl.when(qi == 0)
    def _():
        near(False)

    def far(sub, first_tile):
        span = pl.ds(sub * rows, rows)
        q = q_stack(sub)

        def unfinished(c):
            j, nearest = c
            return jnp.logical_and(j >= 0, nearest < SB_UNDERFLOW_BITS)

        def body(c):
            j, _ = c
            z, drop, total = stage_a(q, j, False)
            run = run_ref[span, :]
            acc_ref[span, :] += stage_c(stage_b(z, drop, run), j)
            run = run + total
            run_ref[span, :] = run
            return j - 1, jnp.min(run)
        lax.while_loop(unfinished, body, (first_tile, jnp.min(run_ref[span, :])))

    far(0, 2 * qi - 2)
    far(1, 2 * qi - 1)
    acc = acc_ref[...]
    y = jnp.concatenate([jnp.where(first, acc[:tile], acc[tile:rows]),
                         jnp.where(first, acc[rows:rows + tile], acc[rows + tile:])], axis=0)
    o_ref[...] = (y * _silu(gate_ref[...].astype(F32))).astype(BF16)


def _sb_attention(proj, batch, seq, *, tile=256):
    tq = 2 * tile
    nq = seq // tq
    pairs = SB_WIDTH // LANES
    c0 = COL_SB // LANES
    idx = jnp.arange(tile)
    suffix = (idx[:, None] >= idx[None, :]).astype(BF16)
    return pl.pallas_call(
        functools.partial(_sb_kernel, tile=tile),
        out_shape=jax.ShapeDtypeStruct((batch * seq, SB_WIDTH), BF16),
        grid=(batch, pairs, nq),
        in_specs=[
            pl.BlockSpec((tq, LANES), lambda b, p, i: (b * nq + i, c0 + p)),
            pl.BlockSpec((seq, LANES), lambda b, p, i: (b, c0 + pairs + p)),
            pl.BlockSpec((seq, LANES), lambda b, p, i: (b, c0 + 2 * pairs + p)),
            pl.BlockSpec((tq, LANES), lambda b, p, i: (b * nq + i, c0 + 3 * pairs + p)),
            pl.BlockSpec((tile, tile), lambda b, p, i: (0, 0)),
        ],
        out_specs=pl.BlockSpec((tq, LANES), lambda b, p, i: (b * nq + i, p)),
        scratch_shapes=[
            pltpu.VMEM((4 * tile, LANES), F32),
            pltpu.VMEM((4 * tile, LANES), F32),
        ],
        compiler_params=pltpu.CompilerParams(
            dimension_semantics=("parallel", "parallel", "arbitrary"), vmem_limit_bytes=VMEM_LIMIT),
        name="sb_attention",
    )(proj, proj, proj, proj, suffix)


def _ssd_kernel(z_ref, x_ref, bc_ref, dt_ref, cwx_ref, cwbc_ref, cbx_ref, cbbc_ref, dtb_ref, ah_ref,
                dskip_ref, ng_ref, tri_ref, e1_ref, e2_ref, o_ref, xpad_ref, bcpad_ref, state_ref, *, chunk):
    ci = pl.program_id(1)
    halo = 8

    @pl.when(ci == 0)
    def _():
        state_ref[...] = jnp.zeros_like(state_ref)
        xpad_ref[0:halo, :] = jnp.zeros((halo, xpad_ref.shape[1]), F32)
        bcpad_ref[0:halo, :] = jnp.zeros((halo, bcpad_ref.shape[1]), F32)

    xpad_ref[halo:halo + chunk, :] = x_ref[...].astype(F32)
    bcpad_ref[halo:halo + chunk, :] = bc_ref[...]

    def conv(pad_ref, w_ref, b_ref):
        acc = b_ref[...]
        for k in range(SSD_CONV):
            off = halo - (SSD_CONV - 1) + k
            acc = acc + w_ref[k:k + 1, :] * pad_ref[off:off + chunk, :]
        return _silu(acc)

    xs = conv(xpad_ref, cwx_ref, cbx_ref)
    bc = conv(bcpad_ref, cwbc_ref, cbbc_ref)
    xpad_ref[0:halo, :] = xpad_ref[chunk:chunk + halo, :]
    bcpad_ref[0:halo, :] = bcpad_ref[chunk:chunk + halo, :]
    bm = bc[:, :LANES]
    cm = bc[:, LANES:]

    dt = _softplus(dt_ref[...] + dtb_ref[...])
    da = dt * ah_ref[...]
    acs = _dot_exact_lhs(tri_ref[...], da)
    e1 = e1_ref[...]
    dt_e = _dot_exact_rhs(dt, e1)
    acs_e = _dot_exact_rhs(acs, e1)
    acs_e2 = _dot_exact_rhs(acs, e2_ref[...])
    acs_t = acs.T

    x_dt = xs * dt_e
    last = acs_e[chunk - 1:chunk, :]
    xs_dec = (x_dt * jnp.exp(last - acs_e)).astype(BF16)

    lane = lax.broadcasted_iota(jnp.int32, (chunk, LANES), 1)
    first = lane < HEAD
    row = lax.broadcasted_iota(jnp.int32, (chunk, chunk), 0)
    col = lax.broadcasted_iota(jnp.int32, (chunk, chunk), 1)
    causal = col <= row
    bm16 = bm.astype(BF16)
    cb = [_dot_nt(jnp.where(first, cm, 0.0).astype(BF16), bm16),
          _dot_nt(jnp.where(first, 0.0, cm).astype(BF16), bm16)]
    x_dt16 = x_dt.astype(BF16)

    heads_per_group = SSD_HEADS // SSD_GROUPS
    pieces = []
    for pair in range(SSD_HEADS // 2):
        xp = x_dt16[:, pair * LANES:(pair + 1) * LANES]
        yd = []
        for h in (2 * pair, 2 * pair + 1):
            seg = acs_e2[:, h * LANES:(h + 1) * LANES] - acs_t[h:h + 1, :]
            m = jnp.where(causal, jnp.exp(seg), 0.0) * cb[h // heads_per_group]
            yd.append(_dot(m.astype(BF16), xp))
        pieces.append(jnp.where(first, yd[0], yd[1]))
    y_diag = jnp.concatenate(pieces, axis=1)

    state = state_ref[...]
    y_off = _dot(cm.astype(BF16), state.astype(BF16)) * jnp.exp(acs_e)
    srow = lax.broadcasted_iota(jnp.int32, state.shape, 0)
    slane = lax.broadcasted_iota(jnp.int32, state.shape, 1)
    same_group = (srow < SSD_STATE) == (slane < SSD_WIDTH // SSD_GROUPS)
    state_ref[...] = state * jnp.exp(last) + jnp.where(same_group, _dot_tn(bm16, xs_dec), 0.0)

    y = y_diag + y_off + xs * dskip_ref[...]
    y = y * _silu(z_ref[...].astype(F32))
    o_ref[...] = _rms_rows(y, ng_ref[...]).astype(BF16)


def _ssd(wide, narrow, batch, seq, conv_w, conv_b, dt_bias, a_log, d_skip, norm_g, *, chunk=128):
    nc = seq // chunk
    nbc = 2 * SSD_GROUPS * SSD_STATE
    pad16 = lambda v: jnp.pad(v, (0, LANES - SSD_HEADS)).reshape(1, LANES)
    idx = jnp.arange(chunk)
    tri = (idx[None, :] <= idx[:, None]).astype(BF16)
    hidx = jnp.arange(LANES)[:, None]
    e1 = (jnp.arange(SSD_WIDTH)[None, :] // HEAD == hidx).astype(BF16)
    e2 = (jnp.arange(SSD_HEADS * LANES)[None, :] // LANES == hidx).astype(BF16)
    full = lambda shape: pl.BlockSpec(shape, lambda b, c: (0, 0))
    row_blk = lambda width, colblk: pl.BlockSpec((chunk, width), lambda b, c: (b * nc + c, colblk))
    return pl.pallas_call(
        functools.partial(_ssd_kernel, chunk=chunk),
        out_shape=jax.ShapeDtypeStruct((batch * seq, SSD_WIDTH), BF16),
        grid=(batch, nc),
        in_specs=[
            row_blk(SSD_WIDTH, COL_Z // SSD_WIDTH),
            row_blk(SSD_WIDTH, COL_X // SSD_WIDTH),
            row_blk(nbc, COL_BC // nbc),
            row_blk(LANES, COL_DT // LANES),
            full((SSD_CONV, SSD_WIDTH)), full((SSD_CONV, nbc)),
            full((1, SSD_WIDTH)), full((1, nbc)),
            full((1, LANES)), full((1, LANES)),
            full((1, SSD_WIDTH)), full((1, SSD_WIDTH)),
            full((chunk, chunk)), full((LANES, SSD_WIDTH)), full((LANES, SSD_HEADS * LANES)),
        ],
        out_specs=pl.BlockSpec((chunk, SSD_WIDTH), lambda b, c: (b * nc + c, 0)),
        scratch_shapes=[
            pltpu.VMEM((chunk + 8, SSD_WIDTH), F32),
            pltpu.VMEM((chunk + 8, nbc), F32),
            pltpu.VMEM((LANES, SSD_WIDTH), F32),
        ],
        compiler_params=pltpu.CompilerParams(
            dimension_semantics=("parallel", "arbitrary"), vmem_limit_bytes=VMEM_LIMIT),
        name="ssd",
    )(wide, wide, narrow, narrow,
      conv_w[:, :SSD_WIDTH], conv_w[:, SSD_WIDTH:],
      conv_b[:SSD_WIDTH].reshape(1, -1), conv_b[SSD_WIDTH:].reshape(1, -1),
      pad16(dt_bias), pad16(-jnp.exp(a_log)),
      jnp.repeat(d_skip, HEAD).reshape(1, -1), norm_g.reshape(1, -1),
      tri, e1, e2)


def _merge_kernel(x_ref, ysb_ref, yssd_ref, yrw_ref, gates_ref, wsb_ref, wssd_ref, wrw_ref, wo_ref, fg_ref,
                  o_ref, *, final_norm):
    g = _sigmoid(gates_ref[...].astype(F32))
    merged = (g[:, :D_MODEL] * _dot(ysb_ref[...], wsb_ref[...])
              + g[:, D_MODEL:2 * D_MODEL] * _dot(yssd_ref[...], wssd_ref[...])
              + g[:, 2 * D_MODEL:] * _dot(yrw_ref[...], wrw_ref[...]))
    out = x_ref[...] + _dot(merged.astype(BF16), wo_ref[...])
    if final_norm:
        out = _rms_rows(out, fg_ref[...])
    o_ref[...] = out


def _merge(x2, y_sb, y_ssd, y_rw, proj, w_sb, w_ssd, w_rw, w_o, final_g, *, final_norm, tm=512):
    t, d = x2.shape
    rows = lambda width, colblk=0: pl.BlockSpec((tm, width), lambda i: (i, colblk))
    full = lambda shape: pl.BlockSpec(shape, lambda i: (0, 0))
    return pl.pallas_call(
        functools.partial(_merge_kernel, final_norm=final_norm),
        out_shape=jax.ShapeDtypeStruct((t, d), F32),
        grid=(t // tm,),
        in_specs=[
            rows(d), rows(SB_WIDTH), rows(SSD_WIDTH), rows(RW_WIDTH),
            rows(3 * d, COL_GATES // (3 * d)),
            full((SB_WIDTH, d)), full((SSD_WIDTH, d)), full((RW_WIDTH, d)), full((d, d)), full((1, d)),
        ],
        out_specs=rows(d),
        compiler_params=pltpu.CompilerParams(
            dimension_semantics=("parallel",), vmem_limit_bytes=VMEM_LIMIT),
        name="merge",
    )(x2, y_sb, y_ssd, y_rw, proj, w_sb, w_ssd, w_rw, w_o, final_g.reshape(1, d))


def _dot2(a, b, dims=None):
    a_hi, a_lo = _split(a, 2)
    b_hi, b_lo = _split(b, 2)
    f = _dot if dims is None else dims
    return f(a_hi, b_hi) + f(a_hi, b_lo) + f(a_lo, b_hi)


def _rwkv_kernel(rkvg_ref, lora_ref, mu_ref, mul_ref, w0_ref, wup_ref, a0_ref, aup_ref, kk_ref, ka_ref, rk_ref,
                 lng_ref, lnb_ref, tri_ref, hsum_ref, o_ref, pad_ref, padl_ref, state_ref, y_ref, *, rows, chunk):
    halo = 8

    @pl.when(pl.program_id(1) == 0)
    def _():
        state_ref[...] = jnp.zeros_like(state_ref)
        pad_ref[0:halo, :] = jnp.zeros((halo, pad_ref.shape[1]), F32)
        padl_ref[0:halo, :] = jnp.zeros((halo, padl_ref.shape[1]), F32)

    def shift_mix(ref, pad, mu):
        cur = ref[...].astype(F32)
        pad[halo:halo + rows, :] = cur
        prev = pad[halo - 1:halo - 1 + rows, :]
        pad[0:halo, :] = pad[rows:rows + halo, :]
        return cur + (prev - cur) * mu[...]

    mixed = shift_mix(rkvg_ref, pad_ref, mu_ref)
    lora = shift_mix(lora_ref, padl_ref, mul_ref)
    w = RW_WIDTH
    r, k, v, gate = (mixed[:, i * w:(i + 1) * w] for i in range(4))
    hsum = hsum_ref[...]

    w_raw = -_softplus(-(w0_ref[...] + _dot2(jnp.tanh(lora), wup_ref[...]))) - 0.5
    log_w = -jnp.exp(w_raw)
    a = _sigmoid(a0_ref[...] + _dot2(lora, aup_ref[...]))
    kk = k * kk_ref[...]
    kk = kk / jnp.maximum(jnp.sqrt(_dot_exact_rhs(kk * kk, hsum, 2)), 1e-12)
    k = k * (1.0 + (a - 1.0) * ka_ref[...])
    bonus = _dot_exact_rhs(r * k * rk_ref[...], hsum, 2) * v
    b = kk * a

    lane = lax.broadcasted_iota(jnp.int32, (chunk, LANES), 1)
    first = lane < HEAD
    n = 2 * chunk
    srow = lax.broadcasted_iota(jnp.int32, (n, n), 0)
    scol = lax.broadcasted_iota(jnp.int32, (n, n), 1)
    strict = scol < srow
    causal = scol <= srow
    eye = (srow == scol).astype(F32)
    drow = lax.broadcasted_iota(jnp.int32, (LANES, LANES), 0)
    dcol = lax.broadcasted_iota(jnp.int32, (LANES, LANES), 1)
    diag = drow == dcol

    nc = rows // chunk
    pairs = RW_WIDTH // LANES
    cl = _dot_exact_lhs(tri_ref[...], log_w)
    cl3 = cl.reshape(nc, chunk, RW_WIDTH)
    last3 = cl3[:, chunk - 1:chunk, :]
    c_tail = jnp.exp(last3 - cl3).reshape(rows, RW_WIDTH)
    c_inv = jnp.exp(-cl)
    c_last = jnp.exp(last3)

    def stacks(x):
        x = x.astype(BF16)
        none = jnp.zeros((chunk, LANES), BF16)
        out = []
        for c in range(nc):
            for p in range(pairs):
                blk = x[c * chunk:(c + 1) * chunk, p * LANES:(p + 1) * LANES]
                out.append(jnp.concatenate([jnp.where(first, blk, none), jnp.where(first, none, blk)], axis=0))
        return jnp.stack(out)

    def bmm(x, y, lhs_dim=2, rhs_dim=1):
        dims = (((lhs_dim,), (rhs_dim,)), ((0,), (0,)))
        return lax.dot_general(x.astype(BF16), y.astype(BF16), dims, preferred_element_type=F32)

    a_s = stacks(-kk * jnp.exp(cl - log_w))
    r_s = stacks(r * jnp.exp(cl))
    b_s = stacks(b * c_inv)
    k_s = stacks(k * c_inv)
    bh_s = stacks(b * c_tail)
    kh_s = stacks(k * c_tail)
    v_s = stacks(v)
    scores = bmm(jnp.concatenate([a_s, r_s], axis=1), jnp.concatenate([b_s, k_s], axis=1), 2, 2)
    a_ab = jnp.where(strict, scores[:, :n, :n], 0.0)
    a_ak = jnp.where(strict, scores[:, :n, n:], 0.0)
    a_rb = jnp.where(causal, scores[:, n:, :n], 0.0).astype(BF16)
    a_rk = jnp.where(causal, scores[:, n:, n:], 0.0)
    x = a_ab
    t = eye + x
    m = 2
    while m < chunk:
        x = bmm(x, x)
        t = t + bmm(t, x)
        m *= 2
    pq1 = bmm(t, jnp.concatenate([a_s, bmm(a_ak, v_s).astype(BF16)], axis=2)).astype(BF16)
    pq2 = bmm(a_rb, pq1)
    p2 = (r_s.astype(F32) + pq2[:, :, :LANES]).astype(BF16)
    q2 = pq2[:, :, LANES:] + bmm(a_rk, v_s)
    gh = bmm(bh_s, pq1, 1, 1)
    g_low = gh[:, :, :LANES]
    h = gh[:, :, LANES:] + bmm(kh_s, v_s, 1, 1)

    for c in range(nc):
        rs = slice(c * chunk, (c + 1) * chunk)
        for p in range(pairs):
            ls = slice(p * LANES, (p + 1) * LANES)
            i = c * pairs + p
            st = state_ref[p]
            st16 = st.astype(BF16)
            ys = _dot(p2[i], st16) + q2[i]
            y_ref[rs, ls] = ys[:chunk] + ys[chunk:]
            g = jnp.where(diag, c_last[c, :, ls], 0.0) + g_low[i]
            state_ref[p] = _dot(g.astype(BF16), st16) + h[i]

    y = y_ref[...]
    inv = 1.0 / HEAD
    mean = _dot_exact_rhs(y, hsum, 2) * inv
    d = y - mean
    var = _dot_exact_rhs(d * d, hsum, 2) * inv
    y = d * lax.rsqrt(var + GN_EPS) * lng_ref[...] + lnb_ref[...] + bonus
    o_ref[...] = (y * _silu(gate)).astype(BF16)


def _rwkv(wide, narrow, batch, seq, mu, w0, w_up, a0, a_up, k_k, k_a, r_k, ln_g, ln_b, *, rows=256, chunk=64):
    nb = seq // rows
    w4 = 4 * RW_WIDTH
    half = RW_LORA // 2
    idx = jnp.arange(rows)
    tri = ((idx[None, :] <= idx[:, None]) & (idx[None, :] // chunk == idx[:, None] // chunk)).astype(BF16)
    lanes = jnp.arange(RW_WIDTH)
    hsum = (lanes[:, None] // HEAD == lanes[None, :] // HEAD).astype(BF16)
    wup_pad = jnp.concatenate([w_up, jnp.zeros((half, RW_WIDTH), F32)], axis=0)
    aup_pad = jnp.concatenate([jnp.zeros((half, RW_WIDTH), F32), a_up], axis=0)
    vec = lambda a: a.reshape(1, -1)
    full = lambda shape: pl.BlockSpec(shape, lambda b, c: (0, 0))
    return pl.pallas_call(
        functools.partial(_rwkv_kernel, rows=rows, chunk=chunk),
        out_shape=jax.ShapeDtypeStruct((batch * seq, RW_WIDTH), BF16),
        grid=(batch, nb),
        in_specs=[
            pl.BlockSpec((rows, w4), lambda b, c: (b * nb + c, COL_RW // w4)),
            pl.BlockSpec((rows, RW_LORA), lambda b, c: (b * nb + c, COL_LORA // RW_LORA)),
            full((1, w4)), full((1, RW_LORA)),
            full((1, RW_WIDTH)), full((RW_LORA, RW_WIDTH)),
            full((1, RW_WIDTH)), full((RW_LORA, RW_WIDTH)),
            full((1, RW_WIDTH)), full((1, RW_WIDTH)), full((1, RW_WIDTH)),
            full((1, RW_WIDTH)), full((1, RW_WIDTH)),
            full((rows, rows)), full((RW_WIDTH, RW_WIDTH)),
        ],
        out_specs=pl.BlockSpec((rows, RW_WIDTH), lambda b, c: (b * nb + c, 0)),
        scratch_shapes=[
            pltpu.VMEM((rows + 8, w4), F32),
            pltpu.VMEM((rows + 8, RW_LORA), F32),
            pltpu.VMEM((RW_WIDTH // LANES, LANES, LANES), F32),
            pltpu.VMEM((rows, RW_WIDTH), F32),
        ],
        compiler_params=pltpu.CompilerParams(
            dimension_semantics=("parallel", "arbitrary"), vmem_limit_bytes=VMEM_LIMIT),
        name="rwkv7",
    )(wide, narrow, vec(mu[:w4]), vec(mu[w4:]), vec(w0), wup_pad, vec(a0), aup_pad, vec(k_k), vec(k_a),
      vec(r_k), vec(ln_g), vec(ln_b), tri, hsum)


def kernel(x, norm_g, w_in, conv_w, conv_b, dt_bias, a_log, d_skip, ssd_norm_g, rw_mu, rw_w0, rw_w_up, rw_a0,
           rw_a_up, rw_k_k, rw_k_a, rw_r_k, rw_ln_g, rw_ln_b, w_out_sb, w_out_ssd, w_out_rw, w_o, final_g):
    batch, seq, d = x.shape
    depth = w_in.shape[0]
    x2 = x.reshape(batch * seq, d)
    for i in range(depth):
        w16 = w_in[i].astype(BF16)
        wide, narrow = _inproj(x2, norm_g[i], _pack_wide(w16), _pack_narrow(w16))
        y_sb = _sb_attention(wide, batch, seq)
        y_ssd = _ssd(wide, narrow, batch, seq, conv_w[i], conv_b[i], dt_bias[i], a_log[i], d_skip[i],
                     ssd_norm_g[i])
        y_rw = _rwkv(wide, narrow, batch, seq, rw_mu[i], rw_w0[i], rw_w_up[i], rw_a0[i], rw_a_up[i], rw_k_k[i],
                     rw_k_a[i], rw_r_k[i].reshape(-1), rw_ln_g[i], rw_ln_b[i])
        x2 = _merge(x2, y_sb, y_ssd, y_rw, wide, w_out_sb[i].astype(BF16), w_out_ssd[i].astype(BF16),
                    w_out_rw[i].astype(BF16), w_o[i].astype(BF16), final_g, final_norm=(i == depth - 1))
    return x2.reshape(batch, seq, d)
```

```python
import functools

import jax
import jax.numpy as jnp
from jax import lax
from jax.experimental import pallas as pl
from jax.experimental.pallas import tpu as pltpu

F32 = jnp.float32
BF16 = jnp.bfloat16

D_MODEL = 1024
HEAD = 64
LANES = 128
SB_WIDTH = 512
SSD_WIDTH = 1024
SSD_HEADS = 16
SSD_GROUPS = 2
SSD_STATE = 64
SSD_CONV = 4
RW_WIDTH = 512
RW_LORA = 128
RMS_EPS = 1e-6
GN_EPS = 64e-5

COL_SB = 0
COL_RW = 2048
COL_Z = 4096
COL_X = 5120
COL_GATES = 6144
N_WIDE = 9216
COL_BC = 0
COL_LORA = 256
COL_DT = 384
N_NARROW = 512

VMEM_LIMIT = 48 * 1024 * 1024


def _dot(a, b):
    return jnp.dot(a, b, preferred_element_type=F32)


def _dot_nt(a, b):
    return lax.dot_general(a, b, (((1,), (1,)), ((), ())), preferred_element_type=F32)


def _dot_tn(a, b):
    return lax.dot_general(a, b, (((0,), (0,)), ((), ())), preferred_element_type=F32)


def _split(x, parts):
    out = []
    for _ in range(parts - 1):
        p = x.astype(BF16)
        out.append(p)
        x = x - p.astype(F32)
    out.append(x.astype(BF16))
    return out


def _dot_exact_rhs(a, b_bf16, parts=3):
    acc = None
    for p in _split(a, parts):
        t = _dot(p, b_bf16)
        acc = t if acc is None else acc + t
    return acc


def _dot_exact_lhs(a_bf16, b, parts=3):
    acc = None
    for p in _split(b, parts):
        t = _dot(a_bf16, p)
        acc = t if acc is None else acc + t
    return acc


def _softplus(x):
    return jnp.maximum(x, 0.0) + jnp.log1p(jnp.exp(-jnp.abs(x)))


def _sigmoid(x):
    return 1.0 / (1.0 + jnp.exp(-x))


def _silu(x):
    return x * _sigmoid(x)


def _rms_rows(x, g):
    ms = jnp.mean(x * x, axis=-1, keepdims=True)
    return x * lax.rsqrt(ms + RMS_EPS) * g


def _pack_wide(w):
    return jnp.concatenate([
        w[..., 0:2048],
        w[..., 4368:6416],
        w[..., 2048:3072],
        w[..., 3072:4096],
        w[..., 6544:9616],
    ], axis=-1)


def _pack_narrow(w):
    pad = jnp.zeros(w.shape[:-1] + (N_NARROW - COL_DT - SSD_HEADS,), w.dtype)
    return jnp.concatenate([
        w[..., 4096:4352],
        w[..., 6416:6544],
        w[..., 4352:4368],
        pad], axis=-1)


def _inproj_kernel(x_ref, g_ref, w_ref, wn_ref, o_ref, on_ref, h_ref, *, rows):
    @pl.when(pl.program_id(1) == 0)
    def _():
        def body(r, c):
            sl = pl.ds(pl.multiple_of(r * rows, rows), rows)
            h_ref[sl, :] = _rms_rows(x_ref[sl, :], g_ref[...]).astype(BF16)
            return c
        lax.fori_loop(0, x_ref.shape[0] // rows, body, 0)
        on_ref[...] = _dot(h_ref[...], wn_ref[...])

    o_ref[...] = _dot(h_ref[...], w_ref[...]).astype(o_ref.dtype)


def _inproj(x2, g, w_wide, w_narrow, *, tm=1024, tn=1024):
    t, d = x2.shape
    n = w_wide.shape[1]
    nn = w_narrow.shape[1]
    return pl.pallas_call(
        functools.partial(_inproj_kernel, rows=min(tm, 256)),
        out_shape=(jax.ShapeDtypeStruct((t, n), BF16), jax.ShapeDtypeStruct((t, nn), F32)),
        grid=(t // tm, n // tn),
        in_specs=[
            pl.BlockSpec((tm, d), lambda i, j: (i, 0)),
            pl.BlockSpec((1, d), lambda i, j: (0, 0)),
            pl.BlockSpec((d, tn), lambda i, j: (0, j)),
            pl.BlockSpec((d, nn), lambda i, j: (0, 0)),
        ],
        out_specs=(pl.BlockSpec((tm, tn), lambda i, j: (i, j)), pl.BlockSpec((tm, nn), lambda i, j: (i, 0))),
        scratch_shapes=[pltpu.VMEM((tm, d), BF16)],
        compiler_params=pltpu.CompilerParams(
            dimension_semantics=("parallel", "arbitrary"), vmem_limit_bytes=VMEM_LIMIT),
        name="inproj",
    )(x2, g.reshape(1, d), w_wide, w_narrow)


SB_MASKED = -1e30
SB_UNDERFLOW_BITS = 160.0
LOG2_E = 1.4426950408889634


def _sb_kernel(q_ref, k_ref, v_ref, gate_ref, suffix_ref, o_ref, acc_ref, run_ref, *, tile):
    qi = pl.program_id(2)
    rows = 2 * tile
    lane = lax.broadcasted_iota(jnp.int32, (tile, LANES), 1)
    first = lane < HEAD
    suffix = suffix_ref[...]
    row = lax.broadcasted_iota(jnp.int32, (rows, tile), 0)
    col = lax.broadcasted_iota(jnp.int32, (rows, tile), 1)
    strict = col < jnp.where(row >= tile, row - tile, row)

    def q_stack(sub):
        q = q_ref[sub * tile:(sub + 1) * tile, :] * jnp.asarray(HEAD ** -0.5, BF16)
        none = jnp.zeros_like(q)
        return jnp.concatenate([jnp.where(first, q, none), jnp.where(first, none, q)], axis=0)

    def keys(j):
        return pl.ds(pl.multiple_of(j * tile, tile), tile)

    def stage_a(q, j, masked):
        z = _dot_nt(q, k_ref[keys(j), :]) * LOG2_E
        drop = jnp.maximum(z, 0.0) + jnp.log2(1.0 + jnp.exp2(-jnp.abs(z)))
        if masked:
            drop = jnp.where(strict, drop, 0.0)
            z = jnp.where(strict, z, SB_MASKED)
        total = jnp.broadcast_to(jnp.sum(drop, axis=-1, keepdims=True), (rows, LANES))
        return z, drop.astype(BF16), total

    def stage_b(z, drop, before):
        x = z - _dot(drop, suffix)
        if before is not None:
            x = x - jnp.concatenate([before] * (tile // LANES), axis=1)
        return jnp.exp2(x).astype(BF16)

    def stage_c(p, j):
        return _dot(p, v_ref[keys(j), :])

    def near(with_previous):
        q_a, q_b = q_stack(0), q_stack(1)
        j = 2 * qi
        units = [(q_a, j, True), (q_b, j + 1, True), (q_b, j, False)]
        if with_previous:
            units.append((q_a, j - 1, False))
        a = [stage_a(*u) for u in units]
        before = [None, None, a[1][2], a[0][2]]
        p = [stage_b(a[i][0], a[i][1], before[i]) for i in range(len(units))]
        c = [stage_c(p[i], units[i][1]) for i in range(len(units))]
        acc_ref[rows:, :] = c[1] + c[2]
        run_ref[rows:, :] = a[1][2] + a[2][2]
        if with_previous:
            acc_ref[:rows, :] = c[0] + c[3]
            run_ref[:rows, :] = a[0][2] + a[3][2]
        else:
            acc_ref[:rows, :] = c[0]
            run_ref[:rows, :] = a[0][2]

    @pl.when(qi > 0)
    def _():
        near(True)

    @pl.when(qi == 0)
    def _():
        near(False)

    def far(sub, first_tile):
        span = pl.ds(sub * rows, rows)
        q = q_stack(sub)

        def unfinished(c):
            j, nearest = c
            return jnp.logical_and(j >= 0, nearest < SB_UNDERFLOW_BITS)

        def body(c):
            j, _ = c
            z, drop, total = stage_a(q, j, False)
            run = run_ref[span, :]
            acc_ref[span, :] += stage_c(stage_b(z, drop, run), j)
            run = run + total
            run_ref[span, :] = run
            return j - 1, jnp.min(run)
        lax.while_loop(unfinished, body, (first_tile, jnp.min(run_ref[span, :])))

    @pl.when(jnp.logical_and(qi > 0, jnp.min(run_ref[...]) < SB_UNDERFLOW_BITS))
    def _():
        far(0, 2 * qi - 2)
        far(1, 2 * qi - 1)

    acc = acc_ref[...]
    y = jnp.concatenate([jnp.where(first, acc[:tile], acc[tile:rows]),
                         jnp.where(first, acc[rows:rows + tile], acc[rows + tile:])], axis=0)
    o_ref[...] = (y * _silu(gate_ref[...].astype(F32))).astype(BF16)


def _sb_attention(proj, batch, seq, *, tile=256):
    tq = 2 * tile
    nq = seq // tq
    pairs = SB_WIDTH // LANES
    c0 = COL_SB // LANES
    idx = jnp.arange(tile)
    suffix = (idx[:, None] >= idx[None, :]).astype(BF16)
    return pl.pallas_call(
        functools.partial(_sb_kernel, tile=tile),
        out_shape=jax.ShapeDtypeStruct((batch * seq, SB_WIDTH), BF16),
        grid=(batch, pairs, nq),
        in_specs=[
            pl.BlockSpec((tq, LANES), lambda b, p, i: (b * nq + i, c0 + p)),
            pl.BlockSpec((seq, LANES), lambda b, p, i: (b, c0 + pairs + p)),
            pl.BlockSpec((seq, LANES), lambda b, p, i: (b, c0 + 2 * pairs + p)),
            pl.BlockSpec((tq, LANES), lambda b, p, i: (b * nq + i, c0 + 3 * pairs + p)),
            pl.BlockSpec((tile, tile), lambda b, p, i: (0, 0)),
        ],
        out_specs=pl.BlockSpec((tq, LANES), lambda b, p, i: (b * nq + i, p)),
        scratch_shapes=[
            pltpu.VMEM((4 * tile, LANES), F32),
            pltpu.VMEM((4 * tile, LANES), F32),
        ],
        compiler_params=pltpu.CompilerParams(
            dimension_semantics=("parallel", "parallel", "arbitrary"), vmem_limit_bytes=VMEM_LIMIT),
        name="sb_attention",
    )(proj, proj, proj, proj, suffix)


def _ssd_kernel(z_ref, x_ref, bc_ref, dt_ref, cwx_ref, cwbc_ref, cbx_ref, cbbc_ref, dtb_ref, ah_ref,
                dskip_ref, ng_ref, tri_ref, e1_ref, e2_ref, o_ref, xpad_ref, bcpad_ref, state_ref, *, chunk,
                sequence_start):
    halo = 8

    if sequence_start is not False:
        @pl.when(sequence_start)
        def _():
            state_ref[...] = jnp.zeros_like(state_ref)
            xpad_ref[0:halo, :] = jnp.zeros((halo, xpad_ref.shape[1]), F32)
            bcpad_ref[0:halo, :] = jnp.zeros((halo, bcpad_ref.shape[1]), F32)

    xpad_ref[halo:halo + chunk, :] = x_ref[...].astype(F32)
    bcpad_ref[halo:halo + chunk, :] = bc_ref[...]

    def conv(pad_ref, w_ref, b_ref):
        acc = b_ref[...]
        for k in range(SSD_CONV):
            off = halo - (SSD_CONV - 1) + k
            acc = acc + w_ref[k:k + 1, :] * pad_ref[off:off + chunk, :]
        return _silu(acc)

    xs = conv(xpad_ref, cwx_ref, cbx_ref)
    bc = conv(bcpad_ref, cwbc_ref, cbbc_ref)
    xpad_ref[0:halo, :] = xpad_ref[chunk:chunk + halo, :]
    bcpad_ref[0:halo, :] = bcpad_ref[chunk:chunk + halo, :]
    bm = bc[:, :LANES]
    cm = bc[:, LANES:]
    yield

    dt = _softplus(dt_ref[...] + dtb_ref[...])
    da = dt * ah_ref[...]
    acs = _dot_exact_lhs(tri_ref[...], da)
    e1 = e1_ref[...]
    dt_e = _dot_exact_rhs(dt, e1)
    acs_e = _dot_exact_rhs(acs, e1)
    acs_e2 = _dot_exact_rhs(acs, e2_ref[...])
    acs_t = acs.T

    x_dt = xs * dt_e
    last = acs_e[chunk - 1:chunk, :]
    xs_dec = (x_dt * jnp.exp(last - acs_e)).astype(BF16)

    lane = lax.broadcasted_iota(jnp.int32, (chunk, LANES), 1)
    first = lane < HEAD
    row = lax.broadcasted_iota(jnp.int32, (chunk, chunk), 0)
    col = lax.broadcasted_iota(jnp.int32, (chunk, chunk), 1)
    causal = col <= row
    bm16 = bm.astype(BF16)
    cb = [_dot_nt(jnp.where(first, cm, 0.0).astype(BF16), bm16),
          _dot_nt(jnp.where(first, 0.0, cm).astype(BF16), bm16)]
    x_dt16 = x_dt.astype(BF16)
    yield

    heads_per_group = SSD_HEADS // SSD_GROUPS
    pieces = []
    for pair in range(SSD_HEADS // 2):
        xp = x_dt16[:, pair * LANES:(pair + 1) * LANES]
        yd = []
        for h in (2 * pair, 2 * pair + 1):
            seg = acs_e2[:, h * LANES:(h + 1) * LANES] - acs_t[h:h + 1, :]
            m = jnp.where(causal, jnp.exp(seg), 0.0) * cb[h // heads_per_group]
            yd.append(_dot(m.astype(BF16), xp))
        pieces.append(jnp.where(first, yd[0], yd[1]))
        yield
    y_diag = jnp.concatenate(pieces, axis=1)

    state = state_ref[...]
    y_off = _dot(cm.astype(BF16), state.astype(BF16)) * jnp.exp(acs_e)
    srow = lax.broadcasted_iota(jnp.int32, state.shape, 0)
    slane = lax.broadcasted_iota(jnp.int32, state.shape, 1)
    same_group = (srow < SSD_STATE) == (slane < SSD_WIDTH // SSD_GROUPS)
    state_ref[...] = state * jnp.exp(last) + jnp.where(same_group, _dot_tn(bm16, xs_dec), 0.0)
    yield

    y = y_diag + y_off + xs * dskip_ref[...]
    y = y * _silu(z_ref[...].astype(F32))
    o_ref[...] = _rms_rows(y, ng_ref[...]).astype(BF16)


SSD_BC = 2 * SSD_GROUPS * SSD_STATE
SSD_CHUNK = 128


def _ssd_constants(conv_w, conv_b, dt_bias, a_log, d_skip, norm_g):
    pad16 = lambda v: jnp.pad(v, (0, LANES - SSD_HEADS)).reshape(1, LANES)
    idx = jnp.arange(SSD_CHUNK)
    tri = (idx[None, :] <= idx[:, None]).astype(BF16)
    hidx = jnp.arange(LANES)[:, None]
    e1 = (jnp.arange(SSD_WIDTH)[None, :] // HEAD == hidx).astype(BF16)
    e2 = (jnp.arange(SSD_HEADS * LANES)[None, :] // LANES == hidx).astype(BF16)
    return [conv_w[:, :SSD_WIDTH], conv_w[:, SSD_WIDTH:],
            conv_b[:SSD_WIDTH].reshape(1, -1), conv_b[SSD_WIDTH:].reshape(1, -1),
            pad16(dt_bias), pad16(-jnp.exp(a_log)),
            jnp.repeat(d_skip, HEAD).reshape(1, -1), norm_g.reshape(1, -1),
            tri, e1, e2]


def _merge_kernel(x_ref, ysb_ref, yssd_ref, yrw_ref, gates_ref, wsb_ref, wssd_ref, wrw_ref, wo_ref, fg_ref,
                  o_ref, *, final_norm):
    g = _sigmoid(gates_ref[...].astype(F32))
    merged = (g[:, :D_MODEL] * _dot(ysb_ref[...], wsb_ref[...])
              + g[:, D_MODEL:2 * D_MODEL] * _dot(yssd_ref[...], wssd_ref[...])
              + g[:, 2 * D_MODEL:] * _dot(yrw_ref[...], wrw_ref[...]))
    out = x_ref[...] + _dot(merged.astype(BF16), wo_ref[...])
    if final_norm:
        out = _rms_rows(out, fg_ref[...])
    o_ref[...] = out


def _merge(x2, y_sb, y_ssd, y_rw, proj, w_sb, w_ssd, w_rw, w_o, final_g, *, final_norm, tm=512):
    t, d = x2.shape
    rows = lambda width, colblk=0: pl.BlockSpec((tm, width), lambda i: (i, colblk))
    full = lambda shape: pl.BlockSpec(shape, lambda i: (0, 0))
    return pl.pallas_call(
        functools.partial(_merge_kernel, final_norm=final_norm),
        out_shape=jax.ShapeDtypeStruct((t, d), F32),
        grid=(t // tm,),
        in_specs=[
            rows(d), rows(SB_WIDTH), rows(SSD_WIDTH), rows(RW_WIDTH),
            rows(3 * d, COL_GATES // (3 * d)),
            full((SB_WIDTH, d)), full((SSD_WIDTH, d)), full((RW_WIDTH, d)), full((d, d)), full((1, d)),
        ],
        out_specs=rows(d),
        compiler_params=pltpu.CompilerParams(
            dimension_semantics=("parallel",), vmem_limit_bytes=VMEM_LIMIT),
        name="merge",
    )(x2, y_sb, y_ssd, y_rw, proj, w_sb, w_ssd, w_rw, w_o, final_g.reshape(1, d))


def _dot2(a, b, dims=None):
    a_hi, a_lo = _split(a, 2)
    b_hi, b_lo = _split(b, 2)
    f = _dot if dims is None else dims
    return f(a_hi, b_hi) + f(a_hi, b_lo) + f(a_lo, b_hi)


def _rwkv_kernel(rkvg_ref, lora_ref, mu_ref, mul_ref, w0_ref, wup_ref, a0_ref, aup_ref, kk_ref, ka_ref, rk_ref,
                 lng_ref, lnb_ref, tri_ref, hsum_ref, o_ref, pad_ref, padl_ref, state_ref, y_ref, *, rows, chunk, group):
    halo = 8

    @pl.when(pl.program_id(1) == 0)
    def _():
        state_ref[...] = jnp.zeros_like(state_ref)
        pad_ref[0:halo, :] = jnp.zeros((halo, pad_ref.shape[1]), F32)
        padl_ref[0:halo, :] = jnp.zeros((halo, padl_ref.shape[1]), F32)

    def shift_mix(ref, pad, mu):
        cur = ref[...].astype(F32)
        pad[halo:halo + rows, :] = cur
        prev = pad[halo - 1:halo - 1 + rows, :]
        pad[0:halo, :] = pad[rows:rows + halo, :]
        return cur + (prev - cur) * mu[...]

    mixed = shift_mix(rkvg_ref, pad_ref, mu_ref)
    lora = shift_mix(lora_ref, padl_ref, mul_ref)
    w = RW_WIDTH
    r, k, v, gate = (mixed[:, i * w:(i + 1) * w] for i in range(4))
    hsum = hsum_ref[...]

    w_raw = -_softplus(-(w0_ref[...] + _dot2(jnp.tanh(lora), wup_ref[...]))) - 0.5
    log_w = -jnp.exp(w_raw)
    a = _sigmoid(a0_ref[...] + _dot2(lora, aup_ref[...]))
    kk = k * kk_ref[...]
    kk = kk / jnp.maximum(jnp.sqrt(_dot_exact_rhs(kk * kk, hsum, 2)), 1e-12)
    k = k * (1.0 + (a - 1.0) * ka_ref[...])
    bonus = _dot_exact_rhs(r * k * rk_ref[...], hsum, 2) * v
    b = kk * a
    yield

    lane = lax.broadcasted_iota(jnp.int32, (chunk, LANES), 1)
    first = lane < HEAD
    n = 2 * chunk
    srow = lax.broadcasted_iota(jnp.int32, (n, n), 0)
    scol = lax.broadcasted_iota(jnp.int32, (n, n), 1)
    strict = scol < srow
    causal = scol <= srow
    eye = (srow == scol).astype(F32)
    drow = lax.broadcasted_iota(jnp.int32, (LANES, LANES), 0)
    dcol = lax.broadcasted_iota(jnp.int32, (LANES, LANES), 1)
    diag = drow == dcol

    nc = rows // chunk
    pairs = RW_WIDTH // LANES
    cl = _dot_exact_lhs(tri_ref[...], log_w)
    cl3 = cl.reshape(nc, chunk, RW_WIDTH)
    last3 = cl3[:, chunk - 1:chunk, :]
    c_tail = jnp.exp(last3 - cl3).reshape(rows, RW_WIDTH)
    c_inv = jnp.exp(-cl)
    c_last = jnp.exp(last3)

    def bmm(x, y, lhs_dim=2, rhs_dim=1):
        dims = (((lhs_dim,), (rhs_dim,)), ((0,), (0,)))
        return lax.dot_general(x.astype(BF16), y.astype(BF16), dims, preferred_element_type=F32)

    def products(c_lo, c_hi):
        def stacks(x):
            x = x.astype(BF16)
            none = jnp.zeros((chunk, LANES), BF16)
            out = []
            for c in range(c_lo, c_hi):
                for p in range(pairs):
                    blk = x[c * chunk:(c + 1) * chunk, p * LANES:(p + 1) * LANES]
                    out.append(jnp.concatenate([jnp.where(first, blk, none), jnp.where(first, none, blk)], axis=0))
            return jnp.stack(out)

        a_s = stacks(-kk * jnp.exp(cl - log_w))
        r_s = stacks(r * jnp.exp(cl))
        b_s = stacks(b * c_inv)
        k_s = stacks(k * c_inv)
        bh_s = stacks(b * c_tail)
        kh_s = stacks(k * c_tail)
        v_s = stacks(v)
        scores = bmm(jnp.concatenate([a_s, r_s], axis=1), jnp.concatenate([b_s, k_s], axis=1), 2, 2)
        a_ab = jnp.where(strict, scores[:, :n, :n], 0.0)
        a_ak = jnp.where(strict, scores[:, :n, n:], 0.0)
        a_rb = jnp.where(causal, scores[:, n:, :n], 0.0).astype(BF16)
        a_rk = jnp.where(causal, scores[:, n:, n:], 0.0)
        yield
        x = a_ab
        t = eye + x
        m = 2
        while m < chunk:
            x = bmm(x, x)
            yield
            t = t + bmm(t, x)
            yield
            m *= 2
        pq1 = bmm(t, jnp.concatenate([a_s, bmm(a_ak, v_s).astype(BF16)], axis=2)).astype(BF16)
        yield
        pq2 = bmm(a_rb, pq1)
        p2 = (r_s.astype(F32) + pq2[:, :, :LANES]).astype(BF16)
        q2 = pq2[:, :, LANES:] + bmm(a_rk, v_s)
        yield
        gh = bmm(bh_s, pq1, 1, 1)
        return p2, q2, gh[:, :, :LANES], gh[:, :, LANES:] + bmm(kh_s, v_s, 1, 1)

    groups = [(c, min(c + group, nc)) for c in range(0, nc, group)]
    prods = []
    for g in groups:
        prods.append((yield from products(*g)))
    for (c_lo, c_hi), (p2, q2, g_low, h) in zip(groups, prods):
        for c in range(c_lo, c_hi):
            rs = slice(c * chunk, (c + 1) * chunk)
            for p in range(pairs):
                ls = slice(p * LANES, (p + 1) * LANES)
                i = (c - c_lo) * pairs + p
                st = state_ref[p]
                st16 = st.astype(BF16)
                ys = _dot(p2[i], st16) + q2[i]
                y_ref[rs, ls] = ys[:chunk] + ys[chunk:]
                g = jnp.where(diag, c_last[c, :, ls], 0.0) + g_low[i]
                state_ref[p] = _dot(g.astype(BF16), st16) + h[i]
            yield

    y = y_ref[...]
    inv = 1.0 / HEAD
    mean = _dot_exact_rhs(y, hsum, 2) * inv
    d = y - mean
    var = _dot_exact_rhs(d * d, hsum, 2) * inv
    y = d * lax.rsqrt(var + GN_EPS) * lng_ref[...] + lnb_ref[...] + bonus
    o_ref[...] = (y * _silu(gate)).astype(BF16)


RW_ROWS = 256
RW_CHUNK = 64


def _rwkv_constants(mu, w0, w_up, a0, a_up, k_k, k_a, r_k, ln_g, ln_b):
    w4 = 4 * RW_WIDTH
    half = RW_LORA // 2
    idx = jnp.arange(RW_ROWS)
    tri = ((idx[None, :] <= idx[:, None]) & (idx[None, :] // RW_CHUNK == idx[:, None] // RW_CHUNK)).astype(BF16)
    lanes = jnp.arange(RW_WIDTH)
    hsum = (lanes[:, None] // HEAD == lanes[None, :] // HEAD).astype(BF16)
    wup_pad = jnp.concatenate([w_up, jnp.zeros((half, RW_WIDTH), F32)], axis=0)
    aup_pad = jnp.concatenate([jnp.zeros((half, RW_WIDTH), F32), a_up], axis=0)
    vec = lambda a: a.reshape(1, -1)
    return [vec(mu[:w4]), vec(mu[w4:]), vec(w0), wup_pad, vec(a0), aup_pad, vec(k_k), vec(k_a),
            vec(r_k), vec(ln_g), vec(ln_b), tri, hsum]


def _recurrent_kernel(*refs, n_ssd, n_rw):
    ssd_in, rw_in = refs[:n_ssd], refs[n_ssd:n_ssd + n_rw]
    o_ssd, o_rw = refs[n_ssd + n_rw:n_ssd + n_rw + 2]
    xpad, bcpad, ssd_state, pad, padl, rw_state, y = refs[n_ssd + n_rw + 2:]
    start = pl.program_id(1) == 0

    def ssd_chunks():
        for c in range(RW_ROWS // SSD_CHUNK):
            win = pl.ds(c * SSD_CHUNK, SSD_CHUNK)
            yield from _ssd_kernel(*(r.at[win] for r in ssd_in[:4]), *ssd_in[4:], o_ssd.at[win], xpad, bcpad,
                                   ssd_state, chunk=SSD_CHUNK, sequence_start=start if c == 0 else False)

    stages = [_rwkv_kernel(*rw_in, o_rw, pad, padl, rw_state, y, rows=RW_ROWS, chunk=RW_CHUNK,
                           group=RW_ROWS // RW_CHUNK), ssd_chunks()]
    while stages:
        for body in list(stages):
            if next(body, stages) is stages:
                stages.remove(body)


def _recurrent_mixers(wide, narrow, batch, seq, ssd_params, rw_params):
    nb = seq // RW_ROWS
    w4 = 4 * RW_WIDTH
    ssd_consts = _ssd_constants(*ssd_params)
    rw_consts = _rwkv_constants(*rw_params)
    rows_blk = lambda width, colblk: pl.BlockSpec((RW_ROWS, width), lambda b, c: (b * nb + c, colblk))
    full = lambda a: pl.BlockSpec(a.shape, lambda b, c: (0,) * a.ndim)
    in_specs = ([rows_blk(SSD_WIDTH, COL_Z // SSD_WIDTH), rows_blk(SSD_WIDTH, COL_X // SSD_WIDTH),
                 rows_blk(SSD_BC, COL_BC // SSD_BC), rows_blk(LANES, COL_DT // LANES)]
                + [full(a) for a in ssd_consts]
                + [rows_blk(w4, COL_RW // w4), rows_blk(RW_LORA, COL_LORA // RW_LORA)]
                + [full(a) for a in rw_consts])
    return pl.pallas_call(
        functools.partial(_recurrent_kernel, n_ssd=4 + len(ssd_consts), n_rw=2 + len(rw_consts)),
        out_shape=(jax.ShapeDtypeStruct((batch * seq, SSD_WIDTH), BF16),
                   jax.ShapeDtypeStruct((batch * seq, RW_WIDTH), BF16)),
        grid=(batch, nb),
        in_specs=in_specs,
        out_specs=(rows_blk(SSD_WIDTH, 0), rows_blk(RW_WIDTH, 0)),
        scratch_shapes=[
            pltpu.VMEM((SSD_CHUNK + 8, SSD_WIDTH), F32),
            pltpu.VMEM((SSD_CHUNK + 8, SSD_BC), F32),
            pltpu.VMEM((LANES, SSD_WIDTH), F32),
            pltpu.VMEM((RW_ROWS + 8, w4), F32),
            pltpu.VMEM((RW_ROWS + 8, RW_LORA), F32),
            pltpu.VMEM((RW_WIDTH // LANES, LANES, LANES), F32),
            pltpu.VMEM((RW_ROWS, RW_WIDTH), F32),
        ],
        compiler_params=pltpu.CompilerParams(
            dimension_semantics=("parallel", "arbitrary"), vmem_limit_bytes=VMEM_LIMIT),
        name="recurrent_mixers",
    )(wide, wide, narrow, narrow, *ssd_consts, wide, narrow, *rw_consts)


def kernel(x, norm_g, w_in, conv_w, conv_b, dt_bias, a_log, d_skip, ssd_norm_g, rw_mu, rw_w0, rw_w_up, rw_a0,
           rw_a_up, rw_k_k, rw_k_a, rw_r_k, rw_ln_g, rw_ln_b, w_out_sb, w_out_ssd, w_out_rw, w_o, final_g):
    batch, seq, d = x.shape
    depth = w_in.shape[0]
    x2 = x.reshape(batch * seq, d)
    for i in range(depth):
        wide, narrow = _inproj(x2, norm_g[i], _pack_wide(w_in[i]).astype(BF16), _pack_narrow(w_in[i]).astype(BF16))
        y_sb = _sb_attention(wide, batch, seq)
        y_ssd, y_rw = _recurrent_mixers(
            wide, narrow, batch, seq,
            (conv_w[i], conv_b[i], dt_bias[i], a_log[i], d_skip[i], ssd_norm_g[i]),
            (rw_mu[i], rw_w0[i], rw_w_up[i], rw_a0[i], rw_a_up[i], rw_k_k[i], rw_k_a[i], rw_r_k[i].reshape(-1),
             rw_ln_g[i], rw_ln_b[i]))
        x2 = _merge(x2, y_sb, y_ssd, y_rw, wide, w_out_sb[i].astype(BF16), w_out_ssd[i].astype(BF16),
                    w_out_rw[i].astype(BF16), w_o[i].astype(BF16), final_g, final_norm=(i == depth - 1))
    return x2.reshape(batch, seq, d)
```

```python
import functools

import jax
import jax.numpy as jnp
from jax import lax
from jax.experimental import pallas as pl
from jax.experimental.pallas import tpu as pltpu

F32 = jnp.float32
BF16 = jnp.bfloat16

D_MODEL = 1024
HEAD = 64
LANES = 128
SB_WIDTH = 512
SSD_WIDTH = 1024
SSD_HEADS = 16
SSD_GROUPS = 2
SSD_STATE = 64
SSD_CONV = 4
RW_WIDTH = 512
RW_LORA = 128
RMS_EPS = 1e-6
GN_EPS = 64e-5

COL_SB = 0
COL_RW = 2048
COL_Z = 4096
COL_X = 5120
COL_GATES = 6144
N_WIDE = 9216
COL_BC = 0
COL_LORA = 256
COL_DT = 384
N_NARROW = 512

VMEM_LIMIT = 48 * 1024 * 1024


def _dot(a, b):
    return jnp.dot(a, b, preferred_element_type=F32)


def _dot_nt(a, b):
    return lax.dot_general(a, b, (((1,), (1,)), ((), ())), preferred_element_type=F32)


def _dot_tn(a, b):
    return lax.dot_general(a, b, (((0,), (0,)), ((), ())), preferred_element_type=F32)


def _split(x, parts):
    out = []
    for _ in range(parts - 1):
        p = x.astype(BF16)
        out.append(p)
        x = x - p.astype(F32)
    out.append(x.astype(BF16))
    return out


def _dot_exact_rhs(a, b_bf16, parts=3):
    acc = None
    for p in _split(a, parts):
        t = _dot(p, b_bf16)
        acc = t if acc is None else acc + t
    return acc


def _dot_exact_lhs(a_bf16, b, parts=3):
    acc = None
    for p in _split(b, parts):
        t = _dot(a_bf16, p)
        acc = t if acc is None else acc + t
    return acc


def _softplus(x):
    return jnp.maximum(x, 0.0) + jnp.log1p(jnp.exp(-jnp.abs(x)))


def _sigmoid(x):
    return 1.0 / (1.0 + jnp.exp(-x))


def _silu(x):
    return x * _sigmoid(x)


def _rms_rows(x, g):
    ms = jnp.mean(x * x, axis=-1, keepdims=True)
    return x * lax.rsqrt(ms + RMS_EPS) * g


def _pack_wide(w):
    return jnp.concatenate([
        w[..., 0:2048],
        w[..., 4368:6416],
        w[..., 2048:3072],
        w[..., 3072:4096],
        w[..., 6544:9616],
    ], axis=-1)


def _pack_narrow(w):
    pad = jnp.zeros(w.shape[:-1] + (N_NARROW - COL_DT - SSD_HEADS,), w.dtype)
    return jnp.concatenate([
        w[..., 4096:4352],
        w[..., 6416:6544],
        w[..., 4352:4368],
        pad], axis=-1)


def _inproj_kernel(x_ref, g_ref, w_ref, wn_ref, o_ref, on_ref, h_ref, *, rows):
    @pl.when(pl.program_id(1) == 0)
    def _():
        def body(r, c):
            sl = pl.ds(pl.multiple_of(r * rows, rows), rows)
            h_ref[sl, :] = _rms_rows(x_ref[sl, :], g_ref[...]).astype(BF16)
            return c
        lax.fori_loop(0, x_ref.shape[0] // rows, body, 0)
        on_ref[...] = _dot(h_ref[...], wn_ref[...])

    o_ref[...] = _dot(h_ref[...], w_ref[...]).astype(o_ref.dtype)


def _inproj(x2, g, w_wide, w_narrow, *, tm=1024, tn=1536):
    t, d = x2.shape
    n = w_wide.shape[1]
    nn = w_narrow.shape[1]
    return pl.pallas_call(
        functools.partial(_inproj_kernel, rows=min(tm, 256)),
        out_shape=(jax.ShapeDtypeStruct((t, n), BF16), jax.ShapeDtypeStruct((t, nn), F32)),
        grid=(t // tm, n // tn),
        in_specs=[
            pl.BlockSpec((tm, d), lambda i, j: (i, 0)),
            pl.BlockSpec((1, d), lambda i, j: (0, 0)),
            pl.BlockSpec((d, tn), lambda i, j: (0, j)),
            pl.BlockSpec((d, nn), lambda i, j: (0, 0)),
        ],
        out_specs=(pl.BlockSpec((tm, tn), lambda i, j: (i, j)), pl.BlockSpec((tm, nn), lambda i, j: (i, 0))),
        scratch_shapes=[pltpu.VMEM((tm, d), BF16)],
        compiler_params=pltpu.CompilerParams(
            dimension_semantics=("parallel", "arbitrary"), vmem_limit_bytes=VMEM_LIMIT),
        name="inproj",
    )(x2, g.reshape(1, d), w_wide, w_narrow)


SB_MASKED = -1e30
SB_UNDERFLOW_BITS = 160.0
LOG2_E = 1.4426950408889634


def _sb_kernel(q_ref, k_ref, v_ref, gate_ref, suffix_ref, o_ref, acc_ref, run_ref, *, tile):
    qi = pl.program_id(2)
    rows = 2 * tile
    lane = lax.broadcasted_iota(jnp.int32, (tile, LANES), 1)
    first = lane < HEAD
    suffix = suffix_ref[...]
    row = lax.broadcasted_iota(jnp.int32, (rows, tile), 0)
    col = lax.broadcasted_iota(jnp.int32, (rows, tile), 1)
    strict = col < jnp.where(row >= tile, row - tile, row)

    def q_stack(sub):
        q = q_ref[sub * tile:(sub + 1) * tile, :] * jnp.asarray(HEAD ** -0.5, BF16)
        none = jnp.zeros_like(q)
        return jnp.concatenate([jnp.where(first, q, none), jnp.where(first, none, q)], axis=0)

    def keys(j):
        return pl.ds(pl.multiple_of(j * tile, tile), tile)

    def stage_a(q, j, masked):
        z = _dot_nt(q, k_ref[keys(j), :]) * LOG2_E
        drop = jnp.maximum(z, 0.0) + jnp.log2(1.0 + jnp.exp2(-jnp.abs(z)))
        if masked:
            drop = jnp.where(strict, drop, 0.0)
            z = jnp.where(strict, z, SB_MASKED)
        total = jnp.broadcast_to(jnp.sum(drop, axis=-1, keepdims=True), (rows, LANES))
        return z, drop.astype(BF16), total

    def stage_b(z, drop, before):
        x = z - _dot(drop, suffix)
        if before is not None:
            x = x - jnp.concatenate([before] * (tile // LANES), axis=1)
        return jnp.exp2(x).astype(BF16)

    def stage_c(p, j):
        return _dot(p, v_ref[keys(j), :])

    def near(with_previous):
        q_a, q_b = q_stack(0), q_stack(1)
        j = 2 * qi
        units = [(q_a, j, True), (q_b, j + 1, True), (q_b, j, False)]
        if with_previous:
            units.append((q_a, j - 1, False))
        a = [stage_a(*u) for u in units]
        before = [None, None, a[1][2], a[0][2]]
        p = [stage_b(a[i][0], a[i][1], before[i]) for i in range(len(units))]
        c = [stage_c(p[i], units[i][1]) for i in range(len(units))]
        acc_ref[rows:, :] = c[1] + c[2]
        run_ref[rows:, :] = a[1][2] + a[2][2]
        if with_previous:
            acc_ref[:rows, :] = c[0] + c[3]
            run_ref[:rows, :] = a[0][2] + a[3][2]
        else:
            acc_ref[:rows, :] = c[0]
            run_ref[:rows, :] = a[0][2]

    @pl.when(qi > 0)
    def _():
        near(True)

    @pl.when(qi == 0)
    def _():
        near(False)

    def far(sub, first_tile):
        span = pl.ds(sub * rows, rows)
        q = q_stack(sub)

        def unfinished(c):
            j, nearest = c
            return jnp.logical_and(j >= 0, nearest < SB_UNDERFLOW_BITS)

        def body(c):
            j, _ = c
            z, drop, total = stage_a(q, j, False)
            run = run_ref[span, :]
            acc_ref[span, :] += stage_c(stage_b(z, drop, run), j)
            run = run + total
            run_ref[span, :] = run
            return j - 1, jnp.min(run)
        lax.while_loop(unfinished, body, (first_tile, jnp.min(run_ref[span, :])))

    @pl.when(jnp.logical_and(qi > 0, jnp.min(run_ref[...]) < SB_UNDERFLOW_BITS))
    def _():
        far(0, 2 * qi - 2)
        far(1, 2 * qi - 1)

    acc = acc_ref[...]
    y = jnp.concatenate([jnp.where(first, acc[:tile], acc[tile:rows]),
                         jnp.where(first, acc[rows:rows + tile], acc[rows + tile:])], axis=0)
    o_ref[...] = (y * _silu(gate_ref[...].astype(F32))).astype(BF16)


def _sb_attention(proj, batch, seq, *, tile=256):
    tq = 2 * tile
    nq = seq // tq
    pairs = SB_WIDTH // LANES
    c0 = COL_SB // LANES
    idx = jnp.arange(tile)
    suffix = (idx[:, None] >= idx[None, :]).astype(BF16)
    return pl.pallas_call(
        functools.partial(_sb_kernel, tile=tile),
        out_shape=jax.ShapeDtypeStruct((batch * seq, SB_WIDTH), BF16),
        grid=(batch, pairs, nq),
        in_specs=[
            pl.BlockSpec((tq, LANES), lambda b, p, i: (b * nq + i, c0 + p)),
            pl.BlockSpec((seq, LANES), lambda b, p, i: (b, c0 + pairs + p)),
            pl.BlockSpec((seq, LANES), lambda b, p, i: (b, c0 + 2 * pairs + p)),
            pl.BlockSpec((tq, LANES), lambda b, p, i: (b * nq + i, c0 + 3 * pairs + p)),
            pl.BlockSpec((tile, tile), lambda b, p, i: (0, 0)),
        ],
        out_specs=pl.BlockSpec((tq, LANES), lambda b, p, i: (b * nq + i, p)),
        scratch_shapes=[
            pltpu.VMEM((4 * tile, LANES), F32),
            pltpu.VMEM((4 * tile, LANES), F32),
        ],
        compiler_params=pltpu.CompilerParams(
            dimension_semantics=("parallel", "parallel", "arbitrary"), vmem_limit_bytes=VMEM_LIMIT),
        name="sb_attention",
    )(proj, proj, proj, proj, suffix)


def _ssd_kernel(z_ref, x_ref, bc_ref, dt_ref, cwx_ref, cwbc_ref, cbx_ref, cbbc_ref, dtb_ref, ah_ref,
                dskip_ref, ng_ref, tri_ref, e1_ref, e2_ref, o_ref, xpad_ref, bcpad_ref, state_ref, *, chunk,
                sequence_start):
    halo = 8

    if sequence_start is not False:
        @pl.when(sequence_start)
        def _():
            state_ref[...] = jnp.zeros_like(state_ref)
            xpad_ref[0:halo, :] = jnp.zeros((halo, xpad_ref.shape[1]), F32)
            bcpad_ref[0:halo, :] = jnp.zeros((halo, bcpad_ref.shape[1]), F32)

    xpad_ref[halo:halo + chunk, :] = x_ref[...].astype(F32)
    bcpad_ref[halo:halo + chunk, :] = bc_ref[...]

    def conv(pad_ref, w_ref, b_ref):
        acc = b_ref[...]
        for k in range(SSD_CONV):
            off = halo - (SSD_CONV - 1) + k
            acc = acc + w_ref[k:k + 1, :] * pad_ref[off:off + chunk, :]
        return _silu(acc)

    xs = conv(xpad_ref, cwx_ref, cbx_ref)
    bc = conv(bcpad_ref, cwbc_ref, cbbc_ref)
    xpad_ref[0:halo, :] = xpad_ref[chunk:chunk + halo, :]
    bcpad_ref[0:halo, :] = bcpad_ref[chunk:chunk + halo, :]
    bm = bc[:, :LANES]
    cm = bc[:, LANES:]
    yield

    dt = _softplus(dt_ref[...] + dtb_ref[...])
    da = dt * ah_ref[...]
    acs = _dot_exact_lhs(tri_ref[...], da)
    e1 = e1_ref[...]
    dt_e = _dot_exact_rhs(dt, e1)
    acs_e = _dot_exact_rhs(acs, e1)
    acs_e2 = _dot_exact_rhs(acs, e2_ref[...])
    acs_t = acs.T

    x_dt = xs * dt_e
    last = acs_e[chunk - 1:chunk, :]
    xs_dec = (x_dt * jnp.exp(last - acs_e)).astype(BF16)

    lane = lax.broadcasted_iota(jnp.int32, (chunk, LANES), 1)
    first = lane < HEAD
    row = lax.broadcasted_iota(jnp.int32, (chunk, chunk), 0)
    col = lax.broadcasted_iota(jnp.int32, (chunk, chunk), 1)
    causal = col <= row
    bm16 = bm.astype(BF16)
    cb = [_dot_nt(jnp.where(first, cm, 0.0).astype(BF16), bm16),
          _dot_nt(jnp.where(first, 0.0, cm).astype(BF16), bm16)]
    x_dt16 = x_dt.astype(BF16)
    yield

    heads_per_group = SSD_HEADS // SSD_GROUPS
    pieces = []
    for pair in range(SSD_HEADS // 2):
        xp = x_dt16[:, pair * LANES:(pair + 1) * LANES]
        yd = []
        for h in (2 * pair, 2 * pair + 1):
            seg = acs_e2[:, h * LANES:(h + 1) * LANES] - acs_t[h:h + 1, :]
            m = jnp.where(causal, jnp.exp(seg), 0.0) * cb[h // heads_per_group]
            yd.append(_dot(m.astype(BF16), xp))
        pieces.append(jnp.where(first, yd[0], yd[1]))
        yield
    y_diag = jnp.concatenate(pieces, axis=1)

    state = state_ref[...]
    y_off = _dot(cm.astype(BF16), state.astype(BF16)) * jnp.exp(acs_e)
    srow = lax.broadcasted_iota(jnp.int32, state.shape, 0)
    slane = lax.broadcasted_iota(jnp.int32, state.shape, 1)
    same_group = (srow < SSD_STATE) == (slane < SSD_WIDTH // SSD_GROUPS)
    state_ref[...] = state * jnp.exp(last) + jnp.where(same_group, _dot_tn(bm16, xs_dec), 0.0)
    yield

    y = y_diag + y_off + xs * dskip_ref[...]
    y = y * _silu(z_ref[...].astype(F32))
    o_ref[...] = _rms_rows(y, ng_ref[...]).astype(BF16)


SSD_BC = 2 * SSD_GROUPS * SSD_STATE
SSD_CHUNK = 128


def _ssd_constants(conv_w, conv_b, dt_bias, a_log, d_skip, norm_g):
    pad16 = lambda v: jnp.pad(v, (0, LANES - SSD_HEADS)).reshape(1, LANES)
    idx = jnp.arange(SSD_CHUNK)
    tri = (idx[None, :] <= idx[:, None]).astype(BF16)
    hidx = jnp.arange(LANES)[:, None]
    e1 = (jnp.arange(SSD_WIDTH)[None, :] // HEAD == hidx).astype(BF16)
    e2 = (jnp.arange(SSD_HEADS * LANES)[None, :] // LANES == hidx).astype(BF16)
    return [conv_w[:, :SSD_WIDTH], conv_w[:, SSD_WIDTH:],
            conv_b[:SSD_WIDTH].reshape(1, -1), conv_b[SSD_WIDTH:].reshape(1, -1),
            pad16(dt_bias), pad16(-jnp.exp(a_log)),
            jnp.repeat(d_skip, HEAD).reshape(1, -1), norm_g.reshape(1, -1),
            tri, e1, e2]


def _merge_kernel(x_ref, ysb_ref, yssd_ref, yrw_ref, gates_ref, wsb_ref, wssd_ref, wrw_ref, wo_ref, fg_ref,
                  o_ref, *, final_norm):
    g = _sigmoid(gates_ref[...].astype(F32))
    merged = (g[:, :D_MODEL] * _dot(ysb_ref[...], wsb_ref[...])
              + g[:, D_MODEL:2 * D_MODEL] * _dot(yssd_ref[...], wssd_ref[...])
              + g[:, 2 * D_MODEL:] * _dot(yrw_ref[...], wrw_ref[...]))
    out = x_ref[...] + _dot(merged.astype(BF16), wo_ref[...])
    if final_norm:
        out = _rms_rows(out, fg_ref[...])
    o_ref[...] = out


def _merge(x2, y_sb, y_ssd, y_rw, proj, w_sb, w_ssd, w_rw, w_o, final_g, *, final_norm, tm=512):
    t, d = x2.shape
    rows = lambda width, colblk=0: pl.BlockSpec((tm, width), lambda i: (i, colblk))
    full = lambda shape: pl.BlockSpec(shape, lambda i: (0, 0))
    return pl.pallas_call(
        functools.partial(_merge_kernel, final_norm=final_norm),
        out_shape=jax.ShapeDtypeStruct((t, d), F32),
        grid=(t // tm,),
        in_specs=[
            rows(d), rows(SB_WIDTH), rows(SSD_WIDTH), rows(RW_WIDTH),
            rows(3 * d, COL_GATES // (3 * d)),
            full((SB_WIDTH, d)), full((SSD_WIDTH, d)), full((RW_WIDTH, d)), full((d, d)), full((1, d)),
        ],
        out_specs=rows(d),
        compiler_params=pltpu.CompilerParams(
            dimension_semantics=("parallel",), vmem_limit_bytes=VMEM_LIMIT),
        name="merge",
    )(x2, y_sb, y_ssd, y_rw, proj, w_sb, w_ssd, w_rw, w_o, final_g.reshape(1, d))


def _dot2(a, b, dims=None):
    a_hi, a_lo = _split(a, 2)
    b_hi, b_lo = _split(b, 2)
    f = _dot if dims is None else dims
    return f(a_hi, b_hi) + f(a_hi, b_lo) + f(a_lo, b_hi)


def _rwkv_kernel(rkvg_ref, lora_ref, mu_ref, mul_ref, w0_ref, wup_ref, a0_ref, aup_ref, kk_ref, ka_ref, rk_ref,
                 lng_ref, lnb_ref, tri_ref, hsum_ref, o_ref, pad_ref, padl_ref, state_ref, y_ref, *, rows, chunk, group):
    halo = 8

    @pl.when(pl.program_id(1) == 0)
    def _():
        state_ref[...] = jnp.zeros_like(state_ref)
        pad_ref[0:halo, :] = jnp.zeros((halo, pad_ref.shape[1]), F32)
        padl_ref[0:halo, :] = jnp.zeros((halo, padl_ref.shape[1]), F32)

    def shift_mix(ref, pad, mu):
        cur = ref[...].astype(F32)
        pad[halo:halo + rows, :] = cur
        prev = pad[halo - 1:halo - 1 + rows, :]
        pad[0:halo, :] = pad[rows:rows + halo, :]
        return cur + (prev - cur) * mu[...]

    mixed = shift_mix(rkvg_ref, pad_ref, mu_ref)
    lora = shift_mix(lora_ref, padl_ref, mul_ref)
    w = RW_WIDTH
    r, k, v, gate = (mixed[:, i * w:(i + 1) * w] for i in range(4))
    hsum = hsum_ref[...]

    w_raw = -_softplus(-(w0_ref[...] + _dot2(jnp.tanh(lora), wup_ref[...]))) - 0.5
    log_w = -jnp.exp(w_raw)
    a = _sigmoid(a0_ref[...] + _dot2(lora, aup_ref[...]))
    kk = k * kk_ref[...]
    kk = kk / jnp.maximum(jnp.sqrt(_dot_exact_rhs(kk * kk, hsum, 2)), 1e-12)
    k = k * (1.0 + (a - 1.0) * ka_ref[...])
    bonus = _dot_exact_rhs(r * k * rk_ref[...], hsum, 2) * v
    b = kk * a
    yield

    lane = lax.broadcasted_iota(jnp.int32, (chunk, LANES), 1)
    first = lane < HEAD
    n = 2 * chunk
    srow = lax.broadcasted_iota(jnp.int32, (n, n), 0)
    scol = lax.broadcasted_iota(jnp.int32, (n, n), 1)
    strict = scol < srow
    causal = scol <= srow
    eye = (srow == scol).astype(F32)
    drow = lax.broadcasted_iota(jnp.int32, (LANES, LANES), 0)
    dcol = lax.broadcasted_iota(jnp.int32, (LANES, LANES), 1)
    diag = drow == dcol

    nc = rows // chunk
    pairs = RW_WIDTH // LANES
    cl = _dot_exact_lhs(tri_ref[...], log_w)
    cl3 = cl.reshape(nc, chunk, RW_WIDTH)
    last3 = cl3[:, chunk - 1:chunk, :]
    c_tail = jnp.exp(last3 - cl3).reshape(rows, RW_WIDTH)
    c_inv = jnp.exp(-cl)
    c_last = jnp.exp(last3)

    def bmm(x, y, lhs_dim=2, rhs_dim=1):
        dims = (((lhs_dim,), (rhs_dim,)), ((0,), (0,)))
        return lax.dot_general(x.astype(BF16), y.astype(BF16), dims, preferred_element_type=F32)

    def products(c_lo, c_hi):
        def stacks(x):
            x = x.astype(BF16)
            none = jnp.zeros((chunk, LANES), BF16)
            out = []
            for c in range(c_lo, c_hi):
                for p in range(pairs):
                    blk = x[c * chunk:(c + 1) * chunk, p * LANES:(p + 1) * LANES]
                    out.append(jnp.concatenate([jnp.where(first, blk, none), jnp.where(first, none, blk)], axis=0))
            return jnp.stack(out)

        a_s = stacks(-kk * jnp.exp(cl - log_w))
        r_s = stacks(r * jnp.exp(cl))
        b_s = stacks(b * c_inv)
        k_s = stacks(k * c_inv)
        bh_s = stacks(b * c_tail)
        kh_s = stacks(k * c_tail)
        v_s = stacks(v)
        scores = bmm(jnp.concatenate([a_s, r_s], axis=1), jnp.concatenate([b_s, k_s], axis=1), 2, 2)
        a_ab = jnp.where(strict, scores[:, :n, :n], 0.0)
        a_ak = jnp.where(strict, scores[:, :n, n:], 0.0)
        a_rb = jnp.where(causal, scores[:, n:, :n], 0.0).astype(BF16)
        a_rk = jnp.where(causal, scores[:, n:, n:], 0.0)
        yield
        t = eye + a_ab
        x = a_ab.astype(BF16)
        x = bmm(x, x).astype(BF16)
        power = 2
        while 2 * power < chunk:
            both = bmm(jnp.concatenate([t.astype(BF16), x], axis=1), x)
            t = t + both[:, :n]
            x = both[:, n:].astype(BF16)
            power *= 2
            yield
        t = t + bmm(t, x)
        yield
        akv = bmm(jnp.concatenate([a_ak.astype(BF16), a_rk.astype(BF16)], axis=1), v_s)
        pq1 = bmm(t, jnp.concatenate([a_s, akv[:, :n].astype(BF16)], axis=2)).astype(BF16)
        yield
        pq2 = bmm(a_rb, pq1)
        p2 = (r_s.astype(F32) + pq2[:, :, :LANES]).astype(BF16)
        q2 = pq2[:, :, LANES:] + akv[:, n:]
        yield
        gh = bmm(bh_s, pq1, 1, 1)
        return p2, q2, gh[:, :, :LANES], gh[:, :, LANES:] + bmm(kh_s, v_s, 1, 1)

    groups = [(c, min(c + group, nc)) for c in range(0, nc, group)]
    prods = []
    for g in groups:
        prods.append((yield from products(*g)))
    for (c_lo, c_hi), (p2, q2, g_low, h) in zip(groups, prods):
        for c in range(c_lo, c_hi):
            rs = slice(c * chunk, (c + 1) * chunk)
            for p in range(pairs):
                ls = slice(p * LANES, (p + 1) * LANES)
                i = (c - c_lo) * pairs + p
                st = state_ref[p]
                st16 = st.astype(BF16)
                g = jnp.where(diag, c_last[c, :, ls], 0.0) + g_low[i]
                both = _dot(jnp.concatenate([p2[i], g.astype(BF16)], axis=0), st16)
                ys = both[:n] + q2[i]
                y_ref[rs, ls] = ys[:chunk] + ys[chunk:]
                state_ref[p] = both[n:] + h[i]
            yield

    y = y_ref[...]
    inv = 1.0 / HEAD
    mean = _dot_exact_rhs(y, hsum, 2) * inv
    d = y - mean
    var = _dot_exact_rhs(d * d, hsum, 2) * inv
    y = d * lax.rsqrt(var + GN_EPS) * lng_ref[...] + lnb_ref[...] + bonus
    o_ref[...] = (y * _silu(gate)).astype(BF16)


RW_ROWS = 256
RW_CHUNK = 64


def _rwkv_constants(mu, w0, w_up, a0, a_up, k_k, k_a, r_k, ln_g, ln_b):
    w4 = 4 * RW_WIDTH
    half = RW_LORA // 2
    idx = jnp.arange(RW_ROWS)
    tri = ((idx[None, :] <= idx[:, None]) & (idx[None, :] // RW_CHUNK == idx[:, None] // RW_CHUNK)).astype(BF16)
    lanes = jnp.arange(RW_WIDTH)
    hsum = (lanes[:, None] // HEAD == lanes[None, :] // HEAD).astype(BF16)
    wup_pad = jnp.concatenate([w_up, jnp.zeros((half, RW_WIDTH), F32)], axis=0)
    aup_pad = jnp.concatenate([jnp.zeros((half, RW_WIDTH), F32), a_up], axis=0)
    vec = lambda a: a.reshape(1, -1)
    return [vec(mu[:w4]), vec(mu[w4:]), vec(w0), wup_pad, vec(a0), aup_pad, vec(k_k), vec(k_a),
            vec(r_k), vec(ln_g), vec(ln_b), tri, hsum]


def _recurrent_kernel(*refs, n_ssd, n_rw):
    ssd_in, rw_in = refs[:n_ssd], refs[n_ssd:n_ssd + n_rw]
    o_ssd, o_rw = refs[n_ssd + n_rw:n_ssd + n_rw + 2]
    xpad, bcpad, ssd_state, pad, padl, rw_state, y = refs[n_ssd + n_rw + 2:]
    start = pl.program_id(1) == 0

    def ssd_chunks():
        for c in range(RW_ROWS // SSD_CHUNK):
            win = pl.ds(c * SSD_CHUNK, SSD_CHUNK)
            yield from _ssd_kernel(*(r.at[win] for r in ssd_in[:4]), *ssd_in[4:], o_ssd.at[win], xpad, bcpad,
                                   ssd_state, chunk=SSD_CHUNK, sequence_start=start if c == 0 else False)

    stages = [_rwkv_kernel(*rw_in, o_rw, pad, padl, rw_state, y, rows=RW_ROWS, chunk=RW_CHUNK,
                           group=RW_ROWS // RW_CHUNK), ssd_chunks()]
    while stages:
        for body in list(stages):
            if next(body, stages) is stages:
                stages.remove(body)


def _recurrent_mixers(wide, narrow, batch, seq, ssd_params, rw_params):
    nb = seq // RW_ROWS
    w4 = 4 * RW_WIDTH
    ssd_consts = _ssd_constants(*ssd_params)
    rw_consts = _rwkv_constants(*rw_params)
    rows_blk = lambda width, colblk: pl.BlockSpec((RW_ROWS, width), lambda b, c: (b * nb + c, colblk))
    full = lambda a: pl.BlockSpec(a.shape, lambda b, c: (0,) * a.ndim)
    in_specs = ([rows_blk(SSD_WIDTH, COL_Z // SSD_WIDTH), rows_blk(SSD_WIDTH, COL_X // SSD_WIDTH),
                 rows_blk(SSD_BC, COL_BC // SSD_BC), rows_blk(LANES, COL_DT // LANES)]
                + [full(a) for a in ssd_consts]
                + [rows_blk(w4, COL_RW // w4), rows_blk(RW_LORA, COL_LORA // RW_LORA)]
                + [full(a) for a in rw_consts])
    return pl.pallas_call(
        functools.partial(_recurrent_kernel, n_ssd=4 + len(ssd_consts), n_rw=2 + len(rw_consts)),
        out_shape=(jax.ShapeDtypeStruct((batch * seq, SSD_WIDTH), BF16),
                   jax.ShapeDtypeStruct((batch * seq, RW_WIDTH), BF16)),
        grid=(batch, nb),
        in_specs=in_specs,
        out_specs=(rows_blk(SSD_WIDTH, 0), rows_blk(RW_WIDTH, 0)),
        scratch_shapes=[
            pltpu.VMEM((SSD_CHUNK + 8, SSD_WIDTH), F32),
            pltpu.VMEM((SSD_CHUNK + 8, SSD_BC), F32),
            pltpu.VMEM((LANES, SSD_WIDTH), F32),
            pltpu.VMEM((RW_ROWS + 8, w4), F32),
            pltpu.VMEM((RW_ROWS + 8, RW_LORA), F32),
            pltpu.VMEM((RW_WIDTH // LANES, LANES, LANES), F32),
            pltpu.VMEM((RW_ROWS, RW_WIDTH), F32),
        ],
        compiler_params=pltpu.CompilerParams(
            dimension_semantics=("parallel", "arbitrary"), vmem_limit_bytes=VMEM_LIMIT),
        name="recurrent_mixers",
    )(wide, wide, narrow, narrow, *ssd_consts, wide, narrow, *rw_consts)


def kernel(x, norm_g, w_in, conv_w, conv_b, dt_bias, a_log, d_skip, ssd_norm_g, rw_mu, rw_w0, rw_w_up, rw_a0,
           rw_a_up, rw_k_k, rw_k_a, rw_r_k, rw_ln_g, rw_ln_b, w_out_sb, w_out_ssd, w_out_rw, w_o, final_g):
    batch, seq, d = x.shape
    depth = w_in.shape[0]
    x2 = x.reshape(batch * seq, d)
    for i in range(depth):
        wide, narrow = _inproj(x2, norm_g[i], _pack_wide(w_in[i]).astype(BF16), _pack_narrow(w_in[i]).astype(BF16))
        y_sb = _sb_attention(wide, batch, seq)
        y_ssd, y_rw = _recurrent_mixers(
            wide, narrow, batch, seq,
            (conv_w[i], conv_b[i], dt_bias[i], a_log[i], d_skip[i], ssd_norm_g[i]),
            (rw_mu[i], rw_w0[i], rw_w_up[i], rw_a0[i], rw_a_up[i], rw_k_k[i], rw_k_a[i], rw_r_k[i].reshape(-1),
             rw_ln_g[i], rw_ln_b[i]))
        x2 = _merge(x2, y_sb, y_ssd, y_rw, wide, w_out_sb[i].astype(BF16), w_out_ssd[i].astype(BF16),
                    w_out_rw[i].astype(BF16), w_o[i].astype(BF16), final_g, final_norm=(i == depth - 1))
    return x2.reshape(batch, seq, d)
```

```python
import functools

import jax
import jax.numpy as jnp
from jax import lax
from jax.experimental import pallas as pl
from jax.experimental.pallas import tpu as pltpu

F32 = jnp.float32
BF16 = jnp.bfloat16

D_MODEL = 1024
HEAD = 64
LANES = 128
MXU_TILE = 256
SB_WIDTH = 512
SSD_WIDTH = 1024
SSD_HEADS = 16
SSD_GROUPS = 2
SSD_STATE = 64
SSD_CONV = 4
RW_WIDTH = 512
RW_LORA = 128
RMS_EPS = 1e-6
GN_EPS = 64e-5

COL_SB = 0
COL_RW = 2048
COL_Z = 4096
COL_X = 5120
COL_GATES = 6144
N_WIDE = 9216
COL_BC = 0
COL_LORA = 256
COL_DT = 384
N_NARROW = 512

VMEM_LIMIT = 48 * 1024 * 1024


def _dot(a, b):
    return jnp.dot(a, b, preferred_element_type=F32)


def _dot_nt(a, b):
    return lax.dot_general(a, b, (((1,), (1,)), ((), ())), preferred_element_type=F32)


def _dot_tn(a, b):
    return lax.dot_general(a, b, (((0,), (0,)), ((), ())), preferred_element_type=F32)


def _split(x, parts):
    out = []
    for _ in range(parts - 1):
        p = x.astype(BF16)
        out.append(p)
        x = x - p.astype(F32)
    out.append(x.astype(BF16))
    return out


def _dot_exact_rhs(a, b_bf16, parts=3):
    acc = None
    for p in _split(a, parts):
        t = _dot(p, b_bf16)
        acc = t if acc is None else acc + t
    return acc


def _dot_exact_lhs(a_bf16, b, parts=3):
    acc = None
    for p in _split(b, parts):
        t = _dot(a_bf16, p)
        acc = t if acc is None else acc + t
    return acc


def _softplus(x):
    return jnp.maximum(x, 0.0) + jnp.log1p(jnp.exp(-jnp.abs(x)))


def _sigmoid(x):
    return 1.0 / (1.0 + jnp.exp(-x))


def _silu(x):
    return x * _sigmoid(x)


def _rms_rows(x, g):
    ms = jnp.mean(x * x, axis=-1, keepdims=True)
    return x * lax.rsqrt(ms + RMS_EPS) * g


def _pack_wide(w):
    return jnp.concatenate([
        w[..., 0:2048],
        w[..., 4368:6416],
        w[..., 2048:3072],
        w[..., 3072:4096],
        w[..., 6544:9616],
    ], axis=-1)


def _pack_narrow(w):
    pad = jnp.zeros(w.shape[:-1] + (N_NARROW - COL_DT - SSD_HEADS,), w.dtype)
    return jnp.concatenate([
        w[..., 4096:4352],
        w[..., 6416:6544],
        w[..., 4352:4368],
        pad], axis=-1)


def _inproj_kernel(x_ref, g_ref, w_ref, wn_ref, o_ref, on_ref, h_ref, *, rows):
    @pl.when(pl.program_id(1) == 0)
    def _():
        def body(r, c):
            sl = pl.ds(pl.multiple_of(r * rows, rows), rows)
            h_ref[sl, :] = _rms_rows(x_ref[sl, :], g_ref[...]).astype(BF16)
            return c
        lax.fori_loop(0, x_ref.shape[0] // rows, body, 0)
        on_ref[...] = _dot(h_ref[...], wn_ref[...])

    o_ref[...] = _dot(h_ref[...], w_ref[...]).astype(o_ref.dtype)


def _inproj(x2, g, w_wide, w_narrow, *, tm=1024, tn=1536):
    t, d = x2.shape
    n = w_wide.shape[1]
    nn = w_narrow.shape[1]
    return pl.pallas_call(
        functools.partial(_inproj_kernel, rows=min(tm, 256)),
        out_shape=(jax.ShapeDtypeStruct((t, n), BF16), jax.ShapeDtypeStruct((t, nn), F32)),
        grid=(t // tm, n // tn),
        in_specs=[
            pl.BlockSpec((tm, d), lambda i, j: (i, 0)),
            pl.BlockSpec((1, d), lambda i, j: (0, 0)),
            pl.BlockSpec((d, tn), lambda i, j: (0, j)),
            pl.BlockSpec((d, nn), lambda i, j: (0, 0)),
        ],
        out_specs=(pl.BlockSpec((tm, tn), lambda i, j: (i, j)), pl.BlockSpec((tm, nn), lambda i, j: (i, 0))),
        scratch_shapes=[pltpu.VMEM((tm, d), BF16)],
        compiler_params=pltpu.CompilerParams(
            dimension_semantics=("parallel", "arbitrary"), vmem_limit_bytes=VMEM_LIMIT),
        name="inproj",
    )(x2, g.reshape(1, d), w_wide, w_narrow)


SB_MASKED = -1e30
SB_UNDERFLOW_BITS = 160.0
LOG2_E = 1.4426950408889634


def _sb_kernel(q_ref, k_ref, v_ref, gate_ref, suffix_ref, o_ref, acc_ref, run_ref, *, tile):
    qi = pl.program_id(2)
    rows = 2 * tile
    lane = lax.broadcasted_iota(jnp.int32, (tile, LANES), 1)
    first = lane < HEAD
    suffix = suffix_ref[...]
    row = lax.broadcasted_iota(jnp.int32, (rows, tile), 0)
    col = lax.broadcasted_iota(jnp.int32, (rows, tile), 1)
    strict = col < jnp.where(row >= tile, row - tile, row)

    def q_stack(sub):
        q = q_ref[sub * tile:(sub + 1) * tile, :] * jnp.asarray(HEAD ** -0.5, BF16)
        none = jnp.zeros_like(q)
        return jnp.concatenate([jnp.where(first, q, none), jnp.where(first, none, q)], axis=0)

    def keys(j):
        return pl.ds(pl.multiple_of(j * tile, tile), tile)

    def stage_a(q, j, masked):
        z = _dot_nt(q, k_ref[keys(j), :]) * LOG2_E
        drop = jnp.maximum(z, 0.0) + jnp.log2(1.0 + jnp.exp2(-jnp.abs(z)))
        if masked:
            drop = jnp.where(strict, drop, 0.0)
            z = jnp.where(strict, z, SB_MASKED)
        total = jnp.broadcast_to(jnp.sum(drop, axis=-1, keepdims=True), (rows, LANES))
        return z, drop.astype(BF16), total

    def stage_b(z, drop, before):
        x = z - _dot(drop, suffix)
        if before is not None:
            x = x - jnp.concatenate([before] * (tile // LANES), axis=1)
        return jnp.exp2(x).astype(BF16)

    def stage_c(p, j):
        return _dot(p, v_ref[keys(j), :])

    def near(with_previous):
        q_a, q_b = q_stack(0), q_stack(1)
        j = 2 * qi
        units = [(q_a, j, True), (q_b, j + 1, True), (q_b, j, False)]
        if with_previous:
            units.append((q_a, j - 1, False))
        a = [stage_a(*u) for u in units]
        before = [None, None, a[1][2], a[0][2]]
        p = [stage_b(a[i][0], a[i][1], before[i]) for i in range(len(units))]
        c = [stage_c(p[i], units[i][1]) for i in range(len(units))]
        acc_ref[rows:, :] = c[1] + c[2]
        run_ref[rows:, :] = a[1][2] + a[2][2]
        if with_previous:
            acc_ref[:rows, :] = c[0] + c[3]
            run_ref[:rows, :] = a[0][2] + a[3][2]
        else:
            acc_ref[:rows, :] = c[0]
            run_ref[:rows, :] = a[0][2]

    @pl.when(qi > 0)
    def _():
        near(True)

    @pl.when(qi == 0)
    def _():
        near(False)

    def far(sub, first_tile):
        span = pl.ds(sub * rows, rows)
        q = q_stack(sub)

        def unfinished(c):
            j, nearest = c
            return jnp.logical_and(j >= 0, nearest < SB_UNDERFLOW_BITS)

        def body(c):
            j, _ = c
            z, drop, total = stage_a(q, j, False)
            run = run_ref[span, :]
            acc_ref[span, :] += stage_c(stage_b(z, drop, run), j)
            run = run + total
            run_ref[span, :] = run
            return j - 1, jnp.min(run)
        lax.while_loop(unfinished, body, (first_tile, jnp.min(run_ref[span, :])))

    @pl.when(jnp.logical_and(qi > 0, jnp.min(run_ref[...]) < SB_UNDERFLOW_BITS))
    def _():
        far(0, 2 * qi - 2)
        far(1, 2 * qi - 1)

    acc = acc_ref[...]
    y = jnp.concatenate([jnp.where(first, acc[:tile], acc[tile:rows]),
                         jnp.where(first, acc[rows:rows + tile], acc[rows + tile:])], axis=0)
    o_ref[...] = (y * _silu(gate_ref[...].astype(F32))).astype(BF16)


def _sb_attention(proj, batch, seq, *, tile=256):
    tq = 2 * tile
    nq = seq // tq
    pairs = SB_WIDTH // LANES
    c0 = COL_SB // LANES
    idx = jnp.arange(tile)
    suffix = (idx[:, None] >= idx[None, :]).astype(BF16)
    return pl.pallas_call(
        functools.partial(_sb_kernel, tile=tile),
        out_shape=jax.ShapeDtypeStruct((batch * seq, SB_WIDTH), BF16),
        grid=(batch, pairs, nq),
        in_specs=[
            pl.BlockSpec((tq, LANES), lambda b, p, i: (b * nq + i, c0 + p)),
            pl.BlockSpec((seq, LANES), lambda b, p, i: (b, c0 + pairs + p)),
            pl.BlockSpec((seq, LANES), lambda b, p, i: (b, c0 + 2 * pairs + p)),
            pl.BlockSpec((tq, LANES), lambda b, p, i: (b * nq + i, c0 + 3 * pairs + p)),
            pl.BlockSpec((tile, tile), lambda b, p, i: (0, 0)),
        ],
        out_specs=pl.BlockSpec((tq, LANES), lambda b, p, i: (b * nq + i, p)),
        scratch_shapes=[
            pltpu.VMEM((4 * tile, LANES), F32),
            pltpu.VMEM((4 * tile, LANES), F32),
        ],
        compiler_params=pltpu.CompilerParams(
            dimension_semantics=("parallel", "parallel", "arbitrary"), vmem_limit_bytes=VMEM_LIMIT),
        name="sb_attention",
    )(proj, proj, proj, proj, suffix)


def _ssd_kernel(z_ref, x_ref, bc_ref, dt_ref, cwx_ref, cwbc_ref, cbx_ref, cbbc_ref, dtb_ref, ah_ref,
                dskip_ref, ng_ref, tri_ref, e1_ref, e2_ref, o_ref, xpad_ref, bcpad_ref, state_ref, *, chunk,
                sequence_start):
    halo = 8

    if sequence_start is not False:
        @pl.when(sequence_start)
        def _():
            state_ref[...] = jnp.zeros_like(state_ref)
            xpad_ref[0:halo, :] = jnp.zeros((halo, xpad_ref.shape[1]), F32)
            bcpad_ref[0:halo, :] = jnp.zeros((halo, bcpad_ref.shape[1]), F32)

    xpad_ref[halo:halo + chunk, :] = x_ref[...].astype(F32)
    bcpad_ref[halo:halo + chunk, :] = bc_ref[...]

    def conv(pad_ref, w_ref, b_ref):
        acc = b_ref[...]
        for k in range(SSD_CONV):
            off = halo - (SSD_CONV - 1) + k
            acc = acc + w_ref[k:k + 1, :] * pad_ref[off:off + chunk, :]
        return _silu(acc)

    xs = conv(xpad_ref, cwx_ref, cbx_ref)
    bc = conv(bcpad_ref, cwbc_ref, cbbc_ref)
    xpad_ref[0:halo, :] = xpad_ref[chunk:chunk + halo, :]
    bcpad_ref[0:halo, :] = bcpad_ref[chunk:chunk + halo, :]
    bm = bc[:, :LANES]
    cm = bc[:, LANES:]
    yield

    dt = _softplus(dt_ref[...] + dtb_ref[...])
    da = dt * ah_ref[...]
    acs = _dot_exact_lhs(tri_ref[...], da)
    e1 = e1_ref[...]
    dt_e = _dot_exact_rhs(dt, e1, 2)
    acs_hi, acs_lo = _split(acs, 2)
    acs = acs_hi.astype(F32) + acs_lo.astype(F32)
    acs_e = _dot(acs_hi, e1) + _dot(acs_lo, e1)
    e2 = e2_ref[...]
    acs_e2 = _dot(acs_hi, e2) + _dot(acs_lo, e2)
    acs_t = acs.T

    x_dt = xs * dt_e
    last = acs_e[chunk - 1:chunk, :]
    xs_dec = (x_dt * jnp.exp(last - acs_e)).astype(BF16)

    lane = lax.broadcasted_iota(jnp.int32, (chunk, LANES), 1)
    first = lane < HEAD
    row = lax.broadcasted_iota(jnp.int32, (chunk, chunk), 0)
    col = lax.broadcasted_iota(jnp.int32, (chunk, chunk), 1)
    causal = col <= row
    bm16 = bm.astype(BF16)
    cb = [_dot_nt(jnp.where(first, cm, 0.0).astype(BF16), bm16),
          _dot_nt(jnp.where(first, 0.0, cm).astype(BF16), bm16)]
    x_dt16 = x_dt.astype(BF16)
    yield

    heads_per_group = SSD_HEADS // SSD_GROUPS
    pieces = []
    for pair in range(SSD_HEADS // 2):
        xp = x_dt16[:, pair * LANES:(pair + 1) * LANES]
        yd = []
        for h in (2 * pair, 2 * pair + 1):
            seg = acs_e2[:, h * LANES:(h + 1) * LANES] - acs_t[h:h + 1, :]
            m = jnp.where(causal, jnp.exp(seg), 0.0) * cb[h // heads_per_group]
            yd.append(_dot(m.astype(BF16), xp))
        pieces.append(jnp.where(first, yd[0], yd[1]))
        yield
    y_diag = jnp.concatenate(pieces, axis=1)

    state = state_ref[...]
    y_off = _dot(cm.astype(BF16), state.astype(BF16)) * jnp.exp(acs_e)
    srow = lax.broadcasted_iota(jnp.int32, state.shape, 0)
    slane = lax.broadcasted_iota(jnp.int32, state.shape, 1)
    same_group = (srow < SSD_STATE) == (slane < SSD_WIDTH // SSD_GROUPS)
    state_ref[...] = state * jnp.exp(last) + jnp.where(same_group, _dot_tn(bm16, xs_dec), 0.0)
    yield

    y = y_diag + y_off + xs * dskip_ref[...]
    y = y * _silu(z_ref[...].astype(F32))
    o_ref[...] = _rms_rows(y, ng_ref[...]).astype(BF16)


SSD_BC = 2 * SSD_GROUPS * SSD_STATE
SSD_CHUNK = 128


def _ssd_constants(conv_w, conv_b, dt_bias, a_log, d_skip, norm_g):
    pad16 = lambda v: jnp.pad(v, (0, LANES - SSD_HEADS)).reshape(1, LANES)
    idx = jnp.arange(SSD_CHUNK)
    tri = (idx[None, :] <= idx[:, None]).astype(BF16)
    hidx = jnp.arange(LANES)[:, None]
    e1 = (jnp.arange(SSD_WIDTH)[None, :] // HEAD == hidx).astype(BF16)
    e2 = (jnp.arange(SSD_HEADS * LANES)[None, :] // LANES == hidx).astype(BF16)
    return [conv_w[:, :SSD_WIDTH], conv_w[:, SSD_WIDTH:],
            conv_b[:SSD_WIDTH].reshape(1, -1), conv_b[SSD_WIDTH:].reshape(1, -1),
            pad16(dt_bias), pad16(-jnp.exp(a_log)),
            jnp.repeat(d_skip, HEAD).reshape(1, -1), norm_g.reshape(1, -1),
            tri, e1, e2]


def _merge_kernel(x_ref, ysb_ref, yssd_ref, yrw_ref, gates_ref, wsb_ref, wssd_ref, wrw_ref, wo_ref, fg_ref,
                  o_ref, *, final_norm):
    g = _sigmoid(gates_ref[...].astype(F32))
    merged = (g[:, :D_MODEL] * _dot(ysb_ref[...], wsb_ref[...])
              + g[:, D_MODEL:2 * D_MODEL] * _dot(yssd_ref[...], wssd_ref[...])
              + g[:, 2 * D_MODEL:] * _dot(yrw_ref[...], wrw_ref[...]))
    out = x_ref[...] + _dot(merged.astype(BF16), wo_ref[...])
    if final_norm:
        out = _rms_rows(out, fg_ref[...])
    o_ref[...] = out


def _merge(x2, y_sb, y_ssd, y_rw, proj, w_sb, w_ssd, w_rw, w_o, final_g, *, final_norm, tm=512):
    t, d = x2.shape
    rows = lambda width, colblk=0: pl.BlockSpec((tm, width), lambda i: (i, colblk))
    full = lambda shape: pl.BlockSpec(shape, lambda i: (0, 0))
    return pl.pallas_call(
        functools.partial(_merge_kernel, final_norm=final_norm),
        out_shape=jax.ShapeDtypeStruct((t, d), F32),
        grid=(t // tm,),
        in_specs=[
            rows(d), rows(SB_WIDTH), rows(SSD_WIDTH), rows(RW_WIDTH),
            rows(3 * d, COL_GATES // (3 * d)),
            full((SB_WIDTH, d)), full((SSD_WIDTH, d)), full((RW_WIDTH, d)), full((d, d)), full((1, d)),
        ],
        out_specs=rows(d),
        compiler_params=pltpu.CompilerParams(
            dimension_semantics=("parallel",), vmem_limit_bytes=VMEM_LIMIT),
        name="merge",
    )(x2, y_sb, y_ssd, y_rw, proj, w_sb, w_ssd, w_rw, w_o, final_g.reshape(1, d))


def _dot2(a, b, dims=None):
    a_hi, a_lo = _split(a, 2)
    b_hi, b_lo = _split(b, 2)
    f = _dot if dims is None else dims
    return f(a_hi, b_hi) + f(a_hi, b_lo) + f(a_lo, b_hi)


def _rwkv_kernel(rkvg_ref, lora_ref, mu_ref, mul_ref, w0_ref, wup_ref, a0_ref, aup_ref, kk_ref, ka_ref, rk_ref,
                 lng_ref, lnb_ref, tri_ref, hsum_ref, o_ref, pad_ref, padl_ref, state_ref, y_ref, *, rows, chunk, group):
    halo = 8

    @pl.when(pl.program_id(1) == 0)
    def _():
        state_ref[...] = jnp.zeros_like(state_ref)
        pad_ref[0:halo, :] = jnp.zeros((halo, pad_ref.shape[1]), F32)
        padl_ref[0:halo, :] = jnp.zeros((halo, padl_ref.shape[1]), F32)

    def shift_mix(ref, pad, mu):
        cur = ref[...].astype(F32)
        pad[halo:halo + rows, :] = cur
        prev = pad[halo - 1:halo - 1 + rows, :]
        pad[0:halo, :] = pad[rows:rows + halo, :]
        return cur + (prev - cur) * mu[...]

    mixed = shift_mix(rkvg_ref, pad_ref, mu_ref)
    lora = shift_mix(lora_ref, padl_ref, mul_ref)
    w = RW_WIDTH
    r, k, v, gate = (mixed[:, i * w:(i + 1) * w] for i in range(4))
    hsum = hsum_ref[...]

    def head_sums(x):
        width = hsum.shape[0]
        return jnp.concatenate([_dot_exact_rhs(x[:, i:i + width], hsum, 2) for i in range(0, RW_WIDTH, width)],
                               axis=1)

    w_raw = -_softplus(-(w0_ref[...] + _dot2(jnp.tanh(lora), wup_ref[...]))) - 0.5
    log_w = -jnp.exp(w_raw)
    a = _sigmoid(a0_ref[...] + _dot2(lora, aup_ref[...]))
    kk = k * kk_ref[...]
    kk = kk / jnp.maximum(jnp.sqrt(head_sums(kk * kk)), 1e-12)
    k = k * (1.0 + (a - 1.0) * ka_ref[...])
    bonus = head_sums(r * k * rk_ref[...]) * v
    b = kk * a
    yield

    lane = lax.broadcasted_iota(jnp.int32, (chunk, LANES), 1)
    first = lane < HEAD
    n = 2 * chunk
    srow = lax.broadcasted_iota(jnp.int32, (n, n), 0)
    scol = lax.broadcasted_iota(jnp.int32, (n, n), 1)
    strict = scol < srow
    causal = scol <= srow
    eye = (srow == scol).astype(F32)
    drow = lax.broadcasted_iota(jnp.int32, (LANES, LANES), 0)
    dcol = lax.broadcasted_iota(jnp.int32, (LANES, LANES), 1)
    diag = drow == dcol

    nc = rows // chunk
    pairs = RW_WIDTH // LANES
    cl = _dot_exact_lhs(tri_ref[...], log_w)
    cl3 = cl.reshape(nc, chunk, RW_WIDTH)
    last3 = cl3[:, chunk - 1:chunk, :]
    c_tail = jnp.exp(last3 - cl3).reshape(rows, RW_WIDTH)
    c_inv = jnp.exp(-cl)
    c_last = jnp.exp(last3)

    def bmm(x, y, lhs_dim=2, rhs_dim=1):
        dims = (((lhs_dim,), (rhs_dim,)), ((0,), (0,)))
        return lax.dot_general(x.astype(BF16), y.astype(BF16), dims, preferred_element_type=F32)

    def products(c_lo, c_hi):
        def stacks(x):
            x = x.astype(BF16)
            none = jnp.zeros((chunk, LANES), BF16)
            out = []
            for c in range(c_lo, c_hi):
                for p in range(pairs):
                    blk = x[c * chunk:(c + 1) * chunk, p * LANES:(p + 1) * LANES]
                    out.append(jnp.concatenate([jnp.where(first, blk, none), jnp.where(first, none, blk)], axis=0))
            return jnp.stack(out)

        a_s = stacks(-kk * jnp.exp(cl - log_w))
        r_s = stacks(r * jnp.exp(cl))
        b_s = stacks(b * c_inv)
        k_s = stacks(k * c_inv)
        bh_s = stacks(b * c_tail)
        kh_s = stacks(k * c_tail)
        v_s = stacks(v)
        scores = bmm(jnp.concatenate([a_s, r_s], axis=1), jnp.concatenate([b_s, k_s], axis=1), 2, 2)
        a_ab = jnp.where(strict, scores[:, :n, :n], 0.0)
        a_ak = jnp.where(strict, scores[:, :n, n:], 0.0)
        a_rb = jnp.where(causal, scores[:, n:, :n], 0.0).astype(BF16)
        a_rk = jnp.where(causal, scores[:, n:, n:], 0.0)
        yield
        t = eye + a_ab
        x = a_ab.astype(BF16)
        x = bmm(x, x).astype(BF16)
        power = 2
        while 2 * power < chunk:
            both = bmm(jnp.concatenate([t.astype(BF16), x], axis=1), x)
            t = t + both[:, :n]
            x = both[:, n:].astype(BF16)
            power *= 2
            yield
        t = t + bmm(t, x)
        yield
        akv = bmm(jnp.concatenate([a_ak.astype(BF16), a_rk.astype(BF16)], axis=1), v_s)
        pq1 = bmm(t, jnp.concatenate([a_s, akv[:, :n].astype(BF16)], axis=2)).astype(BF16)
        yield
        pq2 = bmm(a_rb, pq1)
        p2 = (r_s.astype(F32) + pq2[:, :, :LANES]).astype(BF16)
        q2 = pq2[:, :, LANES:] + akv[:, n:]
        yield
        gh = bmm(bh_s, pq1, 1, 1)
        return p2, q2, gh[:, :, :LANES], gh[:, :, LANES:] + bmm(kh_s, v_s, 1, 1)

    groups = [(c, min(c + group, nc)) for c in range(0, nc, group)]
    prods = []
    for g in groups:
        prods.append((yield from products(*g)))
    for (c_lo, c_hi), (p2, q2, g_low, h) in zip(groups, prods):
        for c in range(c_lo, c_hi):
            rs = slice(c * chunk, (c + 1) * chunk)
            for p in range(pairs):
                ls = slice(p * LANES, (p + 1) * LANES)
                i = (c - c_lo) * pairs + p
                st = state_ref[p]
                st16 = st.astype(BF16)
                g = jnp.where(diag, c_last[c, :, ls], 0.0) + g_low[i]
                both = _dot(jnp.concatenate([p2[i], g.astype(BF16)], axis=0), st16)
                ys = both[:n] + q2[i]
                y_ref[rs, ls] = ys[:chunk] + ys[chunk:]
                state_ref[p] = both[n:] + h[i]
            yield

    y = y_ref[...]
    inv = 1.0 / HEAD
    mean = head_sums(y) * inv
    d = y - mean
    var = head_sums(d * d) * inv
    y = d * lax.rsqrt(var + GN_EPS) * lng_ref[...] + lnb_ref[...] + bonus
    o_ref[...] = (y * _silu(gate)).astype(BF16)


RW_ROWS = 256
RW_CHUNK = 64


def _rwkv_constants(mu, w0, w_up, a0, a_up, k_k, k_a, r_k, ln_g, ln_b):
    w4 = 4 * RW_WIDTH
    half = RW_LORA // 2
    idx = jnp.arange(RW_ROWS)
    tri = ((idx[None, :] <= idx[:, None]) & (idx[None, :] // RW_CHUNK == idx[:, None] // RW_CHUNK)).astype(BF16)
    lanes = jnp.arange(MXU_TILE)
    hsum = (lanes[:, None] // HEAD == lanes[None, :] // HEAD).astype(BF16)
    wup_pad = jnp.concatenate([w_up, jnp.zeros((half, RW_WIDTH), F32)], axis=0)
    aup_pad = jnp.concatenate([jnp.zeros((half, RW_WIDTH), F32), a_up], axis=0)
    vec = lambda a: a.reshape(1, -1)
    return [vec(mu[:w4]), vec(mu[w4:]), vec(w0), wup_pad, vec(a0), aup_pad, vec(k_k), vec(k_a),
            vec(r_k), vec(ln_g), vec(ln_b), tri, hsum]


def _recurrent_kernel(*refs, n_ssd, n_rw):
    ssd_in, rw_in = refs[:n_ssd], refs[n_ssd:n_ssd + n_rw]
    o_ssd, o_rw = refs[n_ssd + n_rw:n_ssd + n_rw + 2]
    xpad, bcpad, ssd_state, pad, padl, rw_state, y = refs[n_ssd + n_rw + 2:]
    start = pl.program_id(1) == 0

    def ssd_chunks():
        for c in range(RW_ROWS // SSD_CHUNK):
            win = pl.ds(c * SSD_CHUNK, SSD_CHUNK)
            yield from _ssd_kernel(*(r.at[win] for r in ssd_in[:4]), *ssd_in[4:], o_ssd.at[win], xpad, bcpad,
                                   ssd_state, chunk=SSD_CHUNK, sequence_start=start if c == 0 else False)

    stages = [_rwkv_kernel(*rw_in, o_rw, pad, padl, rw_state, y, rows=RW_ROWS, chunk=RW_CHUNK,
                           group=RW_ROWS // RW_CHUNK), ssd_chunks()]
    while stages:
        for body in list(stages):
            if next(body, stages) is stages:
                stages.remove(body)


def _recurrent_mixers(wide, narrow, batch, seq, ssd_params, rw_params):
    nb = seq // RW_ROWS
    w4 = 4 * RW_WIDTH
    ssd_consts = _ssd_constants(*ssd_params)
    rw_consts = _rwkv_constants(*rw_params)
    rows_blk = lambda width, colblk: pl.BlockSpec((RW_ROWS, width), lambda b, c: (b * nb + c, colblk))
    full = lambda a: pl.BlockSpec(a.shape, lambda b, c: (0,) * a.ndim)
    in_specs = ([rows_blk(SSD_WIDTH, COL_Z // SSD_WIDTH), rows_blk(SSD_WIDTH, COL_X // SSD_WIDTH),
                 rows_blk(SSD_BC, COL_BC // SSD_BC), rows_blk(LANES, COL_DT // LANES)]
                + [full(a) for a in ssd_consts]
                + [rows_blk(w4, COL_RW // w4), rows_blk(RW_LORA, COL_LORA // RW_LORA)]
                + [full(a) for a in rw_consts])
    return pl.pallas_call(
        functools.partial(_recurrent_kernel, n_ssd=4 + len(ssd_consts), n_rw=2 + len(rw_consts)),
        out_shape=(jax.ShapeDtypeStruct((batch * seq, SSD_WIDTH), BF16),
                   jax.ShapeDtypeStruct((batch * seq, RW_WIDTH), BF16)),
        grid=(batch, nb),
        in_specs=in_specs,
        out_specs=(rows_blk(SSD_WIDTH, 0), rows_blk(RW_WIDTH, 0)),
        scratch_shapes=[
            pltpu.VMEM((SSD_CHUNK + 8, SSD_WIDTH), F32),
            pltpu.VMEM((SSD_CHUNK + 8, SSD_BC), F32),
            pltpu.VMEM((LANES, SSD_WIDTH), F32),
            pltpu.VMEM((RW_ROWS + 8, w4), F32),
            pltpu.VMEM((RW_ROWS + 8, RW_LORA), F32),
            pltpu.VMEM((RW_WIDTH // LANES, LANES, LANES), F32),
            pltpu.VMEM((RW_ROWS, RW_WIDTH), F32),
        ],
        compiler_params=pltpu.CompilerParams(
            dimension_semantics=("parallel", "arbitrary"), vmem_limit_bytes=VMEM_LIMIT),
        name="recurrent_mixers",
    )(wide, wide, narrow, narrow, *ssd_consts, wide, narrow, *rw_consts)


def kernel(x, norm_g, w_in, conv_w, conv_b, dt_bias, a_log, d_skip, ssd_norm_g, rw_mu, rw_w0, rw_w_up, rw_a0,
           rw_a_up, rw_k_k, rw_k_a, rw_r_k, rw_ln_g, rw_ln_b, w_out_sb, w_out_ssd, w_out_rw, w_o, final_g):
    batch, seq, d = x.shape
    depth = w_in.shape[0]
    x2 = x.reshape(batch * seq, d)
    for i in range(depth):
        wide, narrow = _inproj(x2, norm_g[i], _pack_wide(w_in[i]).astype(BF16), _pack_narrow(w_in[i]).astype(BF16))
        y_sb = _sb_attention(wide, batch, seq)
        y_ssd, y_rw = _recurrent_mixers(
            wide, narrow, batch, seq,
            (conv_w[i], conv_b[i], dt_bias[i], a_log[i], d_skip[i], ssd_norm_g[i]),
            (rw_mu[i], rw_w0[i], rw_w_up[i], rw_a0[i], rw_a_up[i], rw_k_k[i], rw_k_a[i], rw_r_k[i].reshape(-1),
             rw_ln_g[i], rw_ln_b[i]))
        x2 = _merge(x2, y_sb, y_ssd, y_rw, wide, w_out_sb[i].astype(BF16), w_out_ssd[i].astype(BF16),
                    w_out_rw[i].astype(BF16), w_o[i].astype(BF16), final_g, final_norm=(i == depth - 1))
    return x2.reshape(batch, seq, d)
```

```python
import functools

import jax
import jax.numpy as jnp
from jax import lax
from jax.experimental import pallas as pl
from jax.experimental.pallas import tpu as pltpu

F32 = jnp.float32
BF16 = jnp.bfloat16

D_MODEL = 1024
HEAD = 64
LANES = 128
MXU_TILE = 256
SB_WIDTH = 512
SSD_WIDTH = 1024
SSD_HEADS = 16
SSD_GROUPS = 2
SSD_STATE = 64
SSD_CONV = 4
RW_WIDTH = 512
RW_LORA = 128
RMS_EPS = 1e-6
GN_EPS = 64e-5

COL_SB = 0
COL_RW = 2048
COL_Z = 4096
COL_X = 5120
COL_GATES = 6144
N_WIDE = 9216
COL_BC = 0
COL_LORA = 256
COL_DT = 384
N_NARROW = 512

VMEM_LIMIT = 48 * 1024 * 1024


def _dot(a, b):
    return jnp.dot(a, b, preferred_element_type=F32)


def _dot_nt(a, b):
    return lax.dot_general(a, b, (((1,), (1,)), ((), ())), preferred_element_type=F32)


def _dot_tn(a, b):
    return lax.dot_general(a, b, (((0,), (0,)), ((), ())), preferred_element_type=F32)


def _split(x, parts):
    out = []
    for _ in range(parts - 1):
        p = x.astype(BF16)
        out.append(p)
        x = x - p.astype(F32)
    out.append(x.astype(BF16))
    return out


def _dot_exact_rhs(a, b_bf16, parts=3):
    acc = None
    for p in _split(a, parts):
        t = _dot(p, b_bf16)
        acc = t if acc is None else acc + t
    return acc


def _dot_exact_lhs(a_bf16, b, parts=3):
    acc = None
    for p in _split(b, parts):
        t = _dot(a_bf16, p)
        acc = t if acc is None else acc + t
    return acc


def _softplus(x):
    return jnp.maximum(x, 0.0) + jnp.log(1.0 + jnp.exp(-jnp.abs(x)))


def _sigmoid(x):
    return 1.0 / (1.0 + jnp.exp(-x))


def _silu(x):
    return x * _sigmoid(x)


def _rms_rows(x, g):
    ms = jnp.mean(x * x, axis=-1, keepdims=True)
    return x * lax.rsqrt(ms + RMS_EPS) * g


def _pack_wide(w):
    return jnp.concatenate([
        w[..., 0:2048],
        w[..., 4368:6416],
        w[..., 2048:3072],
        w[..., 3072:4096],
        w[..., 6544:9616],
    ], axis=-1)


def _pack_narrow(w):
    pad = jnp.zeros(w.shape[:-1] + (N_NARROW - COL_DT - SSD_HEADS,), w.dtype)
    return jnp.concatenate([
        w[..., 4096:4352],
        w[..., 6416:6544],
        w[..., 4352:4368],
        pad], axis=-1)


def _repack_kernel(w_ref, wide_ref, narrow_ref):
    wide_ref[...] = _pack_wide(w_ref[0]).astype(BF16)
    narrow_ref[...] = _pack_narrow(w_ref[0]).astype(BF16)


def _repack_weights(w_in, *, rows=128):
    depth, k, n_in = w_in.shape
    return pl.pallas_call(
        _repack_kernel,
        out_shape=(jax.ShapeDtypeStruct((depth * k, N_WIDE), BF16),
                   jax.ShapeDtypeStruct((depth * k, N_NARROW), BF16)),
        grid=(depth, k // rows),
        in_specs=[pl.BlockSpec((1, rows, n_in), lambda d, i: (d, i, 0))],
        out_specs=(pl.BlockSpec((rows, N_WIDE), lambda d, i: (d * (k // rows) + i, 0)),
                   pl.BlockSpec((rows, N_NARROW), lambda d, i: (d * (k // rows) + i, 0))),
        compiler_params=pltpu.CompilerParams(
            dimension_semantics=("parallel", "parallel"), vmem_limit_bytes=VMEM_LIMIT),
        name="repack_weights",
    )(w_in)


def _inproj_kernel(x_ref, g_ref, w_ref, wn_ref, o_ref, on_ref, h_ref, *, rows):
    @pl.when(pl.program_id(1) == 0)
    def _():
        def body(r, c):
            sl = pl.ds(pl.multiple_of(r * rows, rows), rows)
            h_ref[sl, :] = _rms_rows(x_ref[sl, :], g_ref[...]).astype(BF16)
            return c
        lax.fori_loop(0, x_ref.shape[0] // rows, body, 0)
        on_ref[...] = _dot(h_ref[...], wn_ref[...])

    o_ref[...] = _dot(h_ref[...], w_ref[...]).astype(o_ref.dtype)


def _inproj(x2, g, w_wide, w_narrow, layer, *, tm=1024, tn=1536):
    t, d = x2.shape
    n = w_wide.shape[1]
    nn = w_narrow.shape[1]
    return pl.pallas_call(
        functools.partial(_inproj_kernel, rows=min(tm, 256)),
        out_shape=(jax.ShapeDtypeStruct((t, n), BF16), jax.ShapeDtypeStruct((t, nn), F32)),
        grid=(t // tm, n // tn),
        in_specs=[
            pl.BlockSpec((tm, d), lambda i, j: (i, 0)),
            pl.BlockSpec((1, d), lambda i, j: (0, 0)),
            pl.BlockSpec((d, tn), lambda i, j: (layer, j)),
            pl.BlockSpec((d, nn), lambda i, j: (layer, 0)),
        ],
        out_specs=(pl.BlockSpec((tm, tn), lambda i, j: (i, j)), pl.BlockSpec((tm, nn), lambda i, j: (i, 0))),
        scratch_shapes=[pltpu.VMEM((tm, d), BF16)],
        compiler_params=pltpu.CompilerParams(
            dimension_semantics=("parallel", "arbitrary"), vmem_limit_bytes=VMEM_LIMIT),
        name="inproj",
    )(x2, g.reshape(1, d), w_wide, w_narrow)


SB_MASKED = -1e30
SB_UNDERFLOW_BITS = 160.0
LOG2_E = 1.4426950408889634


def _sb_kernel(q_ref, k_ref, v_ref, gate_ref, suffix_ref, o_ref, acc_ref, run_ref, *, tile):
    qi = pl.program_id(2)
    rows = 2 * tile
    lane = lax.broadcasted_iota(jnp.int32, (tile, LANES), 1)
    first = lane < HEAD
    suffix = suffix_ref[...]
    row = lax.broadcasted_iota(jnp.int32, (rows, tile), 0)
    col = lax.broadcasted_iota(jnp.int32, (rows, tile), 1)
    strict = col < jnp.where(row >= tile, row - tile, row)

    def q_stack(sub):
        q = q_ref[sub * tile:(sub + 1) * tile, :] * jnp.asarray(HEAD ** -0.5, BF16)
        none = jnp.zeros_like(q)
        return jnp.concatenate([jnp.where(first, q, none), jnp.where(first, none, q)], axis=0)

    def keys(j):
        return pl.ds(pl.multiple_of(j * tile, tile), tile)

    def stage_a(q, j, masked):
        z = _dot_nt(q, k_ref[keys(j), :]) * LOG2_E
        drop = jnp.maximum(z, 0.0) + jnp.log2(1.0 + jnp.exp2(-jnp.abs(z)))
        if masked:
            drop = jnp.where(strict, drop, 0.0)
            z = jnp.where(strict, z, SB_MASKED)
        total = jnp.broadcast_to(jnp.sum(drop, axis=-1, keepdims=True), (rows, LANES))
        return z, drop.astype(BF16), total

    def stage_b(z, drop, before):
        x = z - _dot(drop, suffix)
        if before is not None:
            x = x - jnp.concatenate([before] * (tile // LANES), axis=1)
        return jnp.exp2(x).astype(BF16)

    def stage_c(p, j):
        return _dot(p, v_ref[keys(j), :])

    def near(with_previous):
        q_a, q_b = q_stack(0), q_stack(1)
        j = 2 * qi
        units = [(q_a, j, True), (q_b, j + 1, True), (q_b, j, False)]
        if with_previous:
            units.append((q_a, j - 1, False))
        a = [stage_a(*u) for u in units]
        before = [None, None, a[1][2], a[0][2]]
        p = [stage_b(a[i][0], a[i][1], before[i]) for i in range(len(units))]
        c = [stage_c(p[i], units[i][1]) for i in range(len(units))]
        acc_ref[rows:, :] = c[1] + c[2]
        run_ref[rows:, :] = a[1][2] + a[2][2]
        if with_previous:
            acc_ref[:rows, :] = c[0] + c[3]
            run_ref[:rows, :] = a[0][2] + a[3][2]
        else:
            acc_ref[:rows, :] = c[0]
            run_ref[:rows, :] = a[0][2]

    @pl.when(qi > 0)
    def _():
        near(True)

    @pl.when(qi == 0)
    def _():
        near(False)

    def far(sub, first_tile):
        span = pl.ds(sub * rows, rows)
        q = q_stack(sub)

        def unfinished(c):
            j, nearest = c
            return jnp.logical_and(j >= 0, nearest < SB_UNDERFLOW_BITS)

        def body(c):
            j, _ = c
            z, drop, total = stage_a(q, j, False)
            run = run_ref[span, :]
            acc_ref[span, :] += stage_c(stage_b(z, drop, run), j)
            run = run + total
            run_ref[span, :] = run
            return j - 1, jnp.min(run)
        lax.while_loop(unfinished, body, (first_tile, jnp.min(run_ref[span, :])))

    @pl.when(jnp.logical_and(qi > 0, jnp.min(run_ref[...]) < SB_UNDERFLOW_BITS))
    def _():
        far(0, 2 * qi - 2)
        far(1, 2 * qi - 1)

    acc = acc_ref[...]
    y = jnp.concatenate([jnp.where(first, acc[:tile], acc[tile:rows]),
                         jnp.where(first, acc[rows:rows + tile], acc[rows + tile:])], axis=0)
    o_ref[...] = (y * _silu(gate_ref[...].astype(F32))).astype(BF16)


def _sb_attention(proj, batch, seq, *, tile=256):
    tq = 2 * tile
    nq = seq // tq
    pairs = SB_WIDTH // LANES
    c0 = COL_SB // LANES
    idx = jnp.arange(tile)
    suffix = (idx[:, None] >= idx[None, :]).astype(BF16)
    return pl.pallas_call(
        functools.partial(_sb_kernel, tile=tile),
        out_shape=jax.ShapeDtypeStruct((batch * seq, SB_WIDTH), BF16),
        grid=(batch, pairs, nq),
        in_specs=[
            pl.BlockSpec((tq, LANES), lambda b, p, i: (b * nq + i, c0 + p)),
            pl.BlockSpec((seq, LANES), lambda b, p, i: (b, c0 + pairs + p)),
            pl.BlockSpec((seq, LANES), lambda b, p, i: (b, c0 + 2 * pairs + p)),
            pl.BlockSpec((tq, LANES), lambda b, p, i: (b * nq + i, c0 + 3 * pairs + p)),
            pl.BlockSpec((tile, tile), lambda b, p, i: (0, 0)),
        ],
        out_specs=pl.BlockSpec((tq, LANES), lambda b, p, i: (b * nq + i, p)),
        scratch_shapes=[
            pltpu.VMEM((4 * tile, LANES), F32),
            pltpu.VMEM((4 * tile, LANES), F32),
        ],
        compiler_params=pltpu.CompilerParams(
            dimension_semantics=("parallel", "parallel", "arbitrary"), vmem_limit_bytes=VMEM_LIMIT),
        name="sb_attention",
    )(proj, proj, proj, proj, suffix)


def _ssd_kernel(z_ref, x_ref, bc_ref, dt_ref, cwx_ref, cwbc_ref, cbx_ref, cbbc_ref, dtb_ref, ah_ref,
                dskip_ref, ng_ref, tri_ref, e1_ref, e2_ref, o_ref, xpad_ref, bcpad_ref, state_ref, *, chunk,
                sequence_start):
    halo = 8

    if sequence_start is not False:
        @pl.when(sequence_start)
        def _():
            state_ref[...] = jnp.zeros_like(state_ref)
            xpad_ref[0:halo, :] = jnp.zeros((halo, xpad_ref.shape[1]), F32)
            bcpad_ref[0:halo, :] = jnp.zeros((halo, bcpad_ref.shape[1]), F32)

    xpad_ref[halo:halo + chunk, :] = x_ref[...].astype(F32)
    bcpad_ref[halo:halo + chunk, :] = bc_ref[...]

    def conv(pad_ref, w_ref, b_ref):
        acc = b_ref[...]
        for k in range(SSD_CONV):
            off = halo - (SSD_CONV - 1) + k
            acc = acc + w_ref[k:k + 1, :] * pad_ref[off:off + chunk, :]
        return _silu(acc)

    xs = conv(xpad_ref, cwx_ref, cbx_ref)
    bc = conv(bcpad_ref, cwbc_ref, cbbc_ref)
    xpad_ref[0:halo, :] = xpad_ref[chunk:chunk + halo, :]
    bcpad_ref[0:halo, :] = bcpad_ref[chunk:chunk + halo, :]
    bm = bc[:, :LANES]
    cm = bc[:, LANES:]
    yield

    dt = _softplus(dt_ref[...] + dtb_ref[...])
    da = dt * ah_ref[...]
    acs = _dot_exact_lhs(tri_ref[...], da)
    e1 = e1_ref[...]
    dt_e = _dot_exact_rhs(dt, e1, 2)
    acs_hi, acs_lo = _split(acs, 2)
    acs = acs_hi.astype(F32) + acs_lo.astype(F32)
    acs_e = _dot(acs_hi, e1) + _dot(acs_lo, e1)
    e2 = e2_ref[...]
    acs_e2 = _dot(acs_hi, e2) + _dot(acs_lo, e2)
    acs_t = acs.T

    x_dt = xs * dt_e
    last = acs_e[chunk - 1:chunk, :]
    xs_dec = (x_dt * jnp.exp(last - acs_e)).astype(BF16)

    lane = lax.broadcasted_iota(jnp.int32, (chunk, LANES), 1)
    first = lane < HEAD
    row = lax.broadcasted_iota(jnp.int32, (chunk, chunk), 0)
    col = lax.broadcasted_iota(jnp.int32, (chunk, chunk), 1)
    causal = col <= row
    bm16 = bm.astype(BF16)
    cb = [_dot_nt(jnp.where(first, cm, 0.0).astype(BF16), bm16),
          _dot_nt(jnp.where(first, 0.0, cm).astype(BF16), bm16)]
    x_dt16 = x_dt.astype(BF16)
    yield

    heads_per_group = SSD_HEADS // SSD_GROUPS
    pieces = []
    for pair in range(SSD_HEADS // 2):
        xp = x_dt16[:, pair * LANES:(pair + 1) * LANES]
        yd = []
        for h in (2 * pair, 2 * pair + 1):
            seg = acs_e2[:, h * LANES:(h + 1) * LANES] - acs_t[h:h + 1, :]
            m = jnp.where(causal, jnp.exp(seg), 0.0) * cb[h // heads_per_group]
            yd.append(_dot(m.astype(BF16), xp))
        pieces.append(jnp.where(first, yd[0], yd[1]))
        yield
    y_diag = jnp.concatenate(pieces, axis=1)

    state = state_ref[...]
    y_off = _dot(cm.astype(BF16), state.astype(BF16)) * jnp.exp(acs_e)
    srow = lax.broadcasted_iota(jnp.int32, state.shape, 0)
    slane = lax.broadcasted_iota(jnp.int32, state.shape, 1)
    same_group = (srow < SSD_STATE) == (slane < SSD_WIDTH // SSD_GROUPS)
    state_ref[...] = state * jnp.exp(last) + jnp.where(same_group, _dot_tn(bm16, xs_dec), 0.0)
    yield

    y = y_diag + y_off + xs * dskip_ref[...]
    y = y * _silu(z_ref[...].astype(F32))
    o_ref[...] = _rms_rows(y, ng_ref[...]).astype(BF16)


SSD_BC = 2 * SSD_GROUPS * SSD_STATE
SSD_CHUNK = 128


def _ssd_constants(conv_w, conv_b, dt_bias, a_log, d_skip, norm_g):
    pad16 = lambda v: jnp.pad(v, (0, LANES - SSD_HEADS)).reshape(1, LANES)
    idx = jnp.arange(SSD_CHUNK)
    tri = (idx[None, :] <= idx[:, None]).astype(BF16)
    hidx = jnp.arange(LANES)[:, None]
    e1 = (jnp.arange(SSD_WIDTH)[None, :] // HEAD == hidx).astype(BF16)
    e2 = (jnp.arange(SSD_HEADS * LANES)[None, :] // LANES == hidx).astype(BF16)
    return [conv_w[:, :SSD_WIDTH], conv_w[:, SSD_WIDTH:],
            conv_b[:SSD_WIDTH].reshape(1, -1), conv_b[SSD_WIDTH:].reshape(1, -1),
            pad16(dt_bias), pad16(-jnp.exp(a_log)),
            jnp.repeat(d_skip, HEAD).reshape(1, -1), norm_g.reshape(1, -1),
            tri, e1, e2]


def _merge_kernel(x_ref, ysb_ref, yssd_ref, yrw_ref, gates_ref, wsb_ref, wssd_ref, wrw_ref, wo_ref, fg_ref,
                  o_ref, *, final_norm):
    g = _sigmoid(gates_ref[...].astype(F32))
    merged = (g[:, :D_MODEL] * _dot(ysb_ref[...], wsb_ref[...])
              + g[:, D_MODEL:2 * D_MODEL] * _dot(yssd_ref[...], wssd_ref[...])
              + g[:, 2 * D_MODEL:] * _dot(yrw_ref[...], wrw_ref[...]))
    out = x_ref[...] + _dot(merged.astype(BF16), wo_ref[...])
    if final_norm:
        out = _rms_rows(out, fg_ref[...])
    o_ref[...] = out


def _merge(x2, y_sb, y_ssd, y_rw, proj, w_sb, w_ssd, w_rw, w_o, final_g, *, final_norm, tm=512):
    t, d = x2.shape
    rows = lambda width, colblk=0: pl.BlockSpec((tm, width), lambda i: (i, colblk))
    full = lambda shape: pl.BlockSpec(shape, lambda i: (0, 0))
    return pl.pallas_call(
        functools.partial(_merge_kernel, final_norm=final_norm),
        out_shape=jax.ShapeDtypeStruct((t, d), F32),
        grid=(t // tm,),
        in_specs=[
            rows(d), rows(SB_WIDTH), rows(SSD_WIDTH), rows(RW_WIDTH),
            rows(3 * d, COL_GATES // (3 * d)),
            full((SB_WIDTH, d)), full((SSD_WIDTH, d)), full((RW_WIDTH, d)), full((d, d)), full((1, d)),
        ],
        out_specs=rows(d),
        compiler_params=pltpu.CompilerParams(
            dimension_semantics=("parallel",), vmem_limit_bytes=VMEM_LIMIT),
        name="merge",
    )(x2, y_sb, y_ssd, y_rw, proj, w_sb, w_ssd, w_rw, w_o, final_g.reshape(1, d))


def _dot2(a, b, dims=None):
    a_hi, a_lo = _split(a, 2)
    b_hi, b_lo = _split(b, 2)
    f = _dot if dims is None else dims
    return f(a_hi, b_hi) + f(a_hi, b_lo) + f(a_lo, b_hi)


def _rwkv_kernel(rkvg_ref, lora_ref, mu_ref, mul_ref, w0_ref, wup_ref, a0_ref, aup_ref, kk_ref, ka_ref, rk_ref,
                 lng_ref, lnb_ref, tri_ref, hsum_ref, o_ref, pad_ref, padl_ref, state_ref, y_ref, *, rows, chunk, group):
    halo = 8

    @pl.when(pl.program_id(1) == 0)
    def _():
        state_ref[...] = jnp.zeros_like(state_ref)
        pad_ref[0:halo, :] = jnp.zeros((halo, pad_ref.shape[1]), F32)
        padl_ref[0:halo, :] = jnp.zeros((halo, padl_ref.shape[1]), F32)

    def shift_mix(ref, pad, mu):
        cur = ref[...].astype(F32)
        pad[halo:halo + rows, :] = cur
        prev = pad[halo - 1:halo - 1 + rows, :]
        pad[0:halo, :] = pad[rows:rows + halo, :]
        return cur + (prev - cur) * mu[...]

    mixed = shift_mix(rkvg_ref, pad_ref, mu_ref)
    lora = shift_mix(lora_ref, padl_ref, mul_ref)
    w = RW_WIDTH
    r, k, v, gate = (mixed[:, i * w:(i + 1) * w] for i in range(4))
    hsum = hsum_ref[...]

    def head_sums(x):
        width = hsum.shape[0]
        return jnp.concatenate([_dot_exact_rhs(x[:, i:i + width], hsum, 2) for i in range(0, RW_WIDTH, width)],
                               axis=1)

    w_raw = -_softplus(-(w0_ref[...] + _dot2(jnp.tanh(lora), wup_ref[...]))) - 0.5
    log_w = -jnp.exp(w_raw)
    a = _sigmoid(a0_ref[...] + _dot2(lora, aup_ref[...]))
    kk = k * kk_ref[...]
    kk = kk * lax.rsqrt(jnp.maximum(head_sums(kk * kk), 1e-24))
    k = k * (1.0 + (a - 1.0) * ka_ref[...])
    bonus = head_sums(r * k * rk_ref[...]) * v
    b = kk * a
    yield

    lane = lax.broadcasted_iota(jnp.int32, (chunk, LANES), 1)
    first = lane < HEAD
    n = 2 * chunk
    srow = lax.broadcasted_iota(jnp.int32, (n, n), 0)
    scol = lax.broadcasted_iota(jnp.int32, (n, n), 1)
    strict = scol < srow
    causal = scol <= srow
    eye = (srow == scol).astype(F32)
    drow = lax.broadcasted_iota(jnp.int32, (LANES, LANES), 0)
    dcol = lax.broadcasted_iota(jnp.int32, (LANES, LANES), 1)
    diag = drow == dcol

    nc = rows // chunk
    pairs = RW_WIDTH // LANES
    cl = _dot_exact_lhs(tri_ref[...], log_w)
    cl3 = cl.reshape(nc, chunk, RW_WIDTH)
    last3 = cl3[:, chunk - 1:chunk, :]
    c_tail = jnp.exp(last3 - cl3).reshape(rows, RW_WIDTH)
    c_inv = jnp.exp(-cl)
    c_last = jnp.exp(last3)

    def bmm(x, y, lhs_dim=2, rhs_dim=1):
        dims = (((lhs_dim,), (rhs_dim,)), ((0,), (0,)))
        return lax.dot_general(x.astype(BF16), y.astype(BF16), dims, preferred_element_type=F32)

    def products(c_lo, c_hi):
        def stacks(x):
            x = x.astype(BF16)
            none = jnp.zeros((chunk, LANES), BF16)
            out = []
            for c in range(c_lo, c_hi):
                for p in range(pairs):
                    blk = x[c * chunk:(c + 1) * chunk, p * LANES:(p + 1) * LANES]
                    out.append(jnp.concatenate([jnp.where(first, blk, none), jnp.where(first, none, blk)], axis=0))
            return jnp.stack(out)

        a_s = stacks(-kk * jnp.exp(cl - log_w))
        r_s = stacks(r * jnp.exp(cl))
        b_s = stacks(b * c_inv)
        k_s = stacks(k * c_inv)
        bh_s = stacks(b * c_tail)
        kh_s = stacks(k * c_tail)
        v_s = stacks(v)
        scores = bmm(jnp.concatenate([a_s, r_s], axis=1), jnp.concatenate([b_s, k_s], axis=1), 2, 2)
        a_ab = jnp.where(strict, scores[:, :n, :n], 0.0)
        a_ak = jnp.where(strict, scores[:, :n, n:], 0.0)
        a_rb = jnp.where(causal, scores[:, n:, :n], 0.0).astype(BF16)
        a_rk = jnp.where(causal, scores[:, n:, n:], 0.0)
        yield
        t = eye + a_ab
        x = a_ab.astype(BF16)
        x = bmm(x, x).astype(BF16)
        power = 2
        while 2 * power < chunk:
            both = bmm(jnp.concatenate([t.astype(BF16), x], axis=1), x)
            t = t + both[:, :n]
            x = both[:, n:].astype(BF16)
            power *= 2
            yield
        t = t + bmm(t, x)
        yield
        akv = bmm(jnp.concatenate([a_ak.astype(BF16), a_rk.astype(BF16)], axis=1), v_s)
        pq1 = bmm(t, jnp.concatenate([a_s, akv[:, :n].astype(BF16)], axis=2)).astype(BF16)
        yield
        pq2 = bmm(a_rb, pq1)
        p2 = (r_s.astype(F32) + pq2[:, :, :LANES]).astype(BF16)
        q2 = pq2[:, :, LANES:] + akv[:, n:]
        yield
        gh = bmm(bh_s, pq1, 1, 1)
        return p2, q2, gh[:, :, :LANES], gh[:, :, LANES:] + bmm(kh_s, v_s, 1, 1)

    groups = [(c, min(c + group, nc)) for c in range(0, nc, group)]
    prods = []
    for g in groups:
        prods.append((yield from products(*g)))
    for (c_lo, c_hi), (p2, q2, g_low, h) in zip(groups, prods):
        for c in range(c_lo, c_hi):
            rs = slice(c * chunk, (c + 1) * chunk)
            for p in range(pairs):
                ls = slice(p * LANES, (p + 1) * LANES)
                i = (c - c_lo) * pairs + p
                st = state_ref[p]
                st16 = st.astype(BF16)
                g = jnp.where(diag, c_last[c, :, ls], 0.0) + g_low[i]
                both = _dot(jnp.concatenate([p2[i], g.astype(BF16)], axis=0), st16)
                ys = both[:n] + q2[i]
                y_ref[rs, ls] = ys[:chunk] + ys[chunk:]
                state_ref[p] = both[n:] + h[i]
            yield

    y = y_ref[...]
    inv = 1.0 / HEAD
    mean = head_sums(y) * inv
    d = y - mean
    var = head_sums(d * d) * inv
    y = d * lax.rsqrt(var + GN_EPS) * lng_ref[...] + lnb_ref[...] + bonus
    o_ref[...] = (y * _silu(gate)).astype(BF16)


RW_ROWS = 256
RW_CHUNK = 64


def _rwkv_constants(mu, w0, w_up, a0, a_up, k_k, k_a, r_k, ln_g, ln_b):
    w4 = 4 * RW_WIDTH
    half = RW_LORA // 2
    idx = jnp.arange(RW_ROWS)
    tri = ((idx[None, :] <= idx[:, None]) & (idx[None, :] // RW_CHUNK == idx[:, None] // RW_CHUNK)).astype(BF16)
    lanes = jnp.arange(MXU_TILE)
    hsum = (lanes[:, None] // HEAD == lanes[None, :] // HEAD).astype(BF16)
    wup_pad = jnp.concatenate([w_up, jnp.zeros((half, RW_WIDTH), F32)], axis=0)
    aup_pad = jnp.concatenate([jnp.zeros((half, RW_WIDTH), F32), a_up], axis=0)
    vec = lambda a: a.reshape(1, -1)
    return [vec(mu[:w4]), vec(mu[w4:]), vec(w0), wup_pad, vec(a0), aup_pad, vec(k_k), vec(k_a),
            vec(r_k), vec(ln_g), vec(ln_b), tri, hsum]


def _recurrent_kernel(*refs, n_ssd, n_rw):
    ssd_in, rw_in = refs[:n_ssd], refs[n_ssd:n_ssd + n_rw]
    o_ssd, o_rw = refs[n_ssd + n_rw:n_ssd + n_rw + 2]
    xpad, bcpad, ssd_state, pad, padl, rw_state, y = refs[n_ssd + n_rw + 2:]
    start = pl.program_id(1) == 0

    def ssd_chunks():
        for c in range(RW_ROWS // SSD_CHUNK):
            win = pl.ds(c * SSD_CHUNK, SSD_CHUNK)
            yield from _ssd_kernel(*(r.at[win] for r in ssd_in[:4]), *ssd_in[4:], o_ssd.at[win], xpad, bcpad,
                                   ssd_state, chunk=SSD_CHUNK, sequence_start=start if c == 0 else False)

    stages = [_rwkv_kernel(*rw_in, o_rw, pad, padl, rw_state, y, rows=RW_ROWS, chunk=RW_CHUNK,
                           group=RW_ROWS // RW_CHUNK), ssd_chunks()]
    while stages:
        for body in list(stages):
            if next(body, stages) is stages:
                stages.remove(body)


def _recurrent_mixers(wide, narrow, batch, seq, ssd_params, rw_params):
    nb = seq // RW_ROWS
    w4 = 4 * RW_WIDTH
    ssd_consts = _ssd_constants(*ssd_params)
    rw_consts = _rwkv_constants(*rw_params)
    rows_blk = lambda width, colblk: pl.BlockSpec((RW_ROWS, width), lambda b, c: (b * nb + c, colblk))
    full = lambda a: pl.BlockSpec(a.shape, lambda b, c: (0,) * a.ndim)
    in_specs = ([rows_blk(SSD_WIDTH, COL_Z // SSD_WIDTH), rows_blk(SSD_WIDTH, COL_X // SSD_WIDTH),
                 rows_blk(SSD_BC, COL_BC // SSD_BC), rows_blk(LANES, COL_DT // LANES)]
                + [full(a) for a in ssd_consts]
                + [rows_blk(w4, COL_RW // w4), rows_blk(RW_LORA, COL_LORA // RW_LORA)]
                + [full(a) for a in rw_consts])
    return pl.pallas_call(
        functools.partial(_recurrent_kernel, n_ssd=4 + len(ssd_consts), n_rw=2 + len(rw_consts)),
        out_shape=(jax.ShapeDtypeStruct((batch * seq, SSD_WIDTH), BF16),
                   jax.ShapeDtypeStruct((batch * seq, RW_WIDTH), BF16)),
        grid=(batch, nb),
        in_specs=in_specs,
        out_specs=(rows_blk(SSD_WIDTH, 0), rows_blk(RW_WIDTH, 0)),
        scratch_shapes=[
            pltpu.VMEM((SSD_CHUNK + 8, SSD_WIDTH), F32),
            pltpu.VMEM((SSD_CHUNK + 8, SSD_BC), F32),
            pltpu.VMEM((LANES, SSD_WIDTH), F32),
            pltpu.VMEM((RW_ROWS + 8, w4), F32),
            pltpu.VMEM((RW_ROWS + 8, RW_LORA), F32),
            pltpu.VMEM((RW_WIDTH // LANES, LANES, LANES), F32),
            pltpu.VMEM((RW_ROWS, RW_WIDTH), F32),
        ],
        compiler_params=pltpu.CompilerParams(
            dimension_semantics=("parallel", "arbitrary"), vmem_limit_bytes=VMEM_LIMIT),
        name="recurrent_mixers",
    )(wide, wide, narrow, narrow, *ssd_consts, wide, narrow, *rw_consts)


def kernel(x, norm_g, w_in, conv_w, conv_b, dt_bias, a_log, d_skip, ssd_norm_g, rw_mu, rw_w0, rw_w_up, rw_a0,
           rw_a_up, rw_k_k, rw_k_a, rw_r_k, rw_ln_g, rw_ln_b, w_out_sb, w_out_ssd, w_out_rw, w_o, final_g):
    batch, seq, d = x.shape
    depth = w_in.shape[0]
    x2 = x.reshape(batch * seq, d)
    w_wide, w_narrow = _repack_weights(w_in)
    for i in range(depth):
        wide, narrow = _inproj(x2, norm_g[i], w_wide, w_narrow, i)
        y_sb = _sb_attention(wide, batch, seq)
        y_ssd, y_rw = _recurrent_mixers(
            wide, narrow, batch, seq,
            (conv_w[i], conv_b[i], dt_bias[i], a_log[i], d_skip[i], ssd_norm_g[i]),
            (rw_mu[i], rw_w0[i], rw_w_up[i], rw_a0[i], rw_a_up[i], rw_k_k[i], rw_k_a[i], rw_r_k[i].reshape(-1),
             rw_ln_g[i], rw_ln_b[i]))
        x2 = _merge(x2, y_sb, y_ssd, y_rw, wide, w_out_sb[i].astype(BF16), w_out_ssd[i].astype(BF16),
                    w_out_rw[i].astype(BF16), w_o[i].astype(BF16), final_g, final_norm=(i == depth - 1))
    return x2.reshape(batch, seq, d)
```

```python
import functools

import jax
import jax.numpy as jnp
from jax import lax
from jax.experimental import pallas as pl
from jax.experimental.pallas import tpu as pltpu

F32 = jnp.float32
BF16 = jnp.bfloat16

D_MODEL = 1024
HEAD = 64
LANES = 128
MXU_TILE = 256
SB_WIDTH = 512
SSD_WIDTH = 1024
SSD_HEADS = 16
SSD_GROUPS = 2
SSD_STATE = 64
SSD_CONV = 4
RW_WIDTH = 512
RW_LORA = 128
RMS_EPS = 1e-6
GN_EPS = 64e-5

COL_SB = 0
COL_RW = 2048
COL_Z = 4096
COL_X = 5120
COL_GATES = 6144
N_WIDE = 9216
COL_BC = 0
COL_LORA = 256
COL_DT = 384
N_NARROW = 512

VMEM_LIMIT = 48 * 1024 * 1024


def _dot(a, b):
    return jnp.dot(a, b, preferred_element_type=F32)


def _dot_nt(a, b):
    return lax.dot_general(a, b, (((1,), (1,)), ((), ())), preferred_element_type=F32)


def _dot_tn(a, b):
    return lax.dot_general(a, b, (((0,), (0,)), ((), ())), preferred_element_type=F32)


def _split(x, parts):
    out = []
    for _ in range(parts - 1):
        p = x.astype(BF16)
        out.append(p)
        x = x - p.astype(F32)
    out.append(x.astype(BF16))
    return out


def _dot_exact_rhs(a, b_bf16, parts=3):
    acc = None
    for p in _split(a, parts):
        t = _dot(p, b_bf16)
        acc = t if acc is None else acc + t
    return acc


def _dot_exact_lhs(a_bf16, b, parts=3):
    acc = None
    for p in _split(b, parts):
        t = _dot(a_bf16, p)
        acc = t if acc is None else acc + t
    return acc


def _softplus(x):
    return jnp.maximum(x, 0.0) + jnp.log(1.0 + jnp.exp(-jnp.abs(x)))


def _sigmoid(x):
    return 1.0 / (1.0 + jnp.exp(-x))


def _silu(x):
    return x * _sigmoid(x)


def _rms_rows(x, g):
    ms = jnp.mean(x * x, axis=-1, keepdims=True)
    return x * lax.rsqrt(ms + RMS_EPS) * g


def _pack_wide(w):
    return jnp.concatenate([
        w[..., 0:2048],
        w[..., 4368:6416],
        w[..., 2048:3072],
        w[..., 3072:4096],
        w[..., 6544:9616],
    ], axis=-1)


def _pack_narrow(w):
    pad = jnp.zeros(w.shape[:-1] + (N_NARROW - COL_DT - SSD_HEADS,), w.dtype)
    return jnp.concatenate([
        w[..., 4096:4352],
        w[..., 6416:6544],
        w[..., 4352:4368],
        pad], axis=-1)


def _inproj_kernel(x_ref, g_ref, w_ref, wn_ref, o_ref, on_ref, h_ref, *, rows):
    @pl.when(pl.program_id(1) == 0)
    def _():
        def body(r, c):
            sl = pl.ds(pl.multiple_of(r * rows, rows), rows)
            h_ref[sl, :] = _rms_rows(x_ref[sl, :], g_ref[...]).astype(BF16)
            return c
        lax.fori_loop(0, x_ref.shape[0] // rows, body, 0)
        on_ref[...] = _dot_nt(h_ref[...], wn_ref[...])

    o_ref[...] = _dot_nt(h_ref[...], w_ref[...]).astype(o_ref.dtype)


def _inproj(x2, g, w_wide, w_narrow, layer, *, tm=1024, tn=1536):
    t, d = x2.shape
    depth = w_narrow.shape[0] // N_NARROW
    n = w_wide.shape[0] // depth
    nn = N_NARROW
    return pl.pallas_call(
        functools.partial(_inproj_kernel, rows=min(tm, 256)),
        out_shape=(jax.ShapeDtypeStruct((t, n), BF16), jax.ShapeDtypeStruct((t, nn), F32)),
        grid=(t // tm, n // tn),
        in_specs=[
            pl.BlockSpec((tm, d), lambda i, j: (i, 0)),
            pl.BlockSpec((1, d), lambda i, j: (0, 0)),
            pl.BlockSpec((tn, d), lambda i, j: (layer * (n // tn) + j, 0)),
            pl.BlockSpec((nn, d), lambda i, j: (layer, 0)),
        ],
        out_specs=(pl.BlockSpec((tm, tn), lambda i, j: (i, j)), pl.BlockSpec((tm, nn), lambda i, j: (i, 0))),
        scratch_shapes=[pltpu.VMEM((tm, d), BF16)],
        compiler_params=pltpu.CompilerParams(
            dimension_semantics=("parallel", "arbitrary"), vmem_limit_bytes=VMEM_LIMIT),
        name="inproj",
    )(x2, g.reshape(1, d), w_wide, w_narrow)


SB_MASKED = -1e30
SB_UNDERFLOW_BITS = 160.0
LOG2_E = 1.4426950408889634


def _sb_kernel(q_ref, k_ref, v_ref, gate_ref, suffix_ref, o_ref, acc_ref, run_ref, *, tile):
    qi = pl.program_id(2)
    rows = 2 * tile
    lane = lax.broadcasted_iota(jnp.int32, (tile, LANES), 1)
    first = lane < HEAD
    suffix = suffix_ref[...]
    row = lax.broadcasted_iota(jnp.int32, (rows, tile), 0)
    col = lax.broadcasted_iota(jnp.int32, (rows, tile), 1)
    strict = col < jnp.where(row >= tile, row - tile, row)

    def q_stack(sub):
        q = q_ref[sub * tile:(sub + 1) * tile, :] * jnp.asarray(HEAD ** -0.5, BF16)
        none = jnp.zeros_like(q)
        return jnp.concatenate([jnp.where(first, q, none), jnp.where(first, none, q)], axis=0)

    def keys(j):
        return pl.ds(pl.multiple_of(j * tile, tile), tile)

    def stage_a(q, j, masked):
        z = _dot_nt(q, k_ref[keys(j), :]) * LOG2_E
        drop = jnp.maximum(z, 0.0) + jnp.log2(1.0 + jnp.exp2(-jnp.abs(z)))
        if masked:
            drop = jnp.where(strict, drop, 0.0)
            z = jnp.where(strict, z, SB_MASKED)
        total = jnp.broadcast_to(jnp.sum(drop, axis=-1, keepdims=True), (rows, LANES))
        return z, drop.astype(BF16), total

    def stage_b(z, drop, before):
        x = z - _dot(drop, suffix)
        if before is not None:
            x = x - jnp.concatenate([before] * (tile // LANES), axis=1)
        return jnp.exp2(x).astype(BF16)

    def stage_c(p, j):
        return _dot(p, v_ref[keys(j), :])

    def near(with_previous):
        q_a, q_b = q_stack(0), q_stack(1)
        j = 2 * qi
        units = [(q_a, j, True), (q_b, j + 1, True), (q_b, j, False)]
        if with_previous:
            units.append((q_a, j - 1, False))
        a = [stage_a(*u) for u in units]
        before = [None, None, a[1][2], a[0][2]]
        p = [stage_b(a[i][0], a[i][1], before[i]) for i in range(len(units))]
        c = [stage_c(p[i], units[i][1]) for i in range(len(units))]
        acc_ref[rows:, :] = c[1] + c[2]
        run_ref[rows:, :] = a[1][2] + a[2][2]
        if with_previous:
            acc_ref[:rows, :] = c[0] + c[3]
            run_ref[:rows, :] = a[0][2] + a[3][2]
        else:
            acc_ref[:rows, :] = c[0]
            run_ref[:rows, :] = a[0][2]

    @pl.when(qi > 0)
    def _():
        near(True)

    @pl.when(qi == 0)
    def _():
        near(False)

    def far(sub, first_tile):
        span = pl.ds(sub * rows, rows)
        q = q_stack(sub)

        def unfinished(c):
            j, nearest = c
            return jnp.logical_and(j >= 0, nearest < SB_UNDERFLOW_BITS)

        def body(c):
            j, _ = c
            z, drop, total = stage_a(q, j, False)
            run = run_ref[span, :]
            acc_ref[span, :] += stage_c(stage_b(z, drop, run), j)
            run = run + total
            run_ref[span, :] = run
            return j - 1, jnp.min(run)
        lax.while_loop(unfinished, body, (first_tile, jnp.min(run_ref[span, :])))

    @pl.when(jnp.logical_and(qi > 0, jnp.min(run_ref[...]) < SB_UNDERFLOW_BITS))
    def _():
        far(0, 2 * qi - 2)
        far(1, 2 * qi - 1)

    acc = acc_ref[...]
    y = jnp.concatenate([jnp.where(first, acc[:tile], acc[tile:rows]),
                         jnp.where(first, acc[rows:rows + tile], acc[rows + tile:])], axis=0)
    o_ref[...] = (y * _silu(gate_ref[...].astype(F32))).astype(BF16)


def _sb_attention(proj, batch, seq, *, tile=256):
    tq = 2 * tile
    nq = seq // tq
    pairs = SB_WIDTH // LANES
    c0 = COL_SB // LANES
    idx = jnp.arange(tile)
    suffix = (idx[:, None] >= idx[None, :]).astype(BF16)
    return pl.pallas_call(
        functools.partial(_sb_kernel, tile=tile),
        out_shape=jax.ShapeDtypeStruct((batch * seq, SB_WIDTH), BF16),
        grid=(batch, pairs, nq),
        in_specs=[
            pl.BlockSpec((tq, LANES), lambda b, p, i: (b * nq + i, c0 + p)),
            pl.BlockSpec((seq, LANES), lambda b, p, i: (b, c0 + pairs + p)),
            pl.BlockSpec((seq, LANES), lambda b, p, i: (b, c0 + 2 * pairs + p)),
            pl.BlockSpec((tq, LANES), lambda b, p, i: (b * nq + i, c0 + 3 * pairs + p)),
            pl.BlockSpec((tile, tile), lambda b, p, i: (0, 0)),
        ],
        out_specs=pl.BlockSpec((tq, LANES), lambda b, p, i: (b * nq + i, p)),
        scratch_shapes=[
            pltpu.VMEM((4 * tile, LANES), F32),
            pltpu.VMEM((4 * tile, LANES), F32),
        ],
        compiler_params=pltpu.CompilerParams(
            dimension_semantics=("parallel", "parallel", "arbitrary"), vmem_limit_bytes=VMEM_LIMIT),
        name="sb_attention",
    )(proj, proj, proj, proj, suffix)


def _ssd_kernel(z_ref, x_ref, bc_ref, dt_ref, cwx_ref, cwbc_ref, cbx_ref, cbbc_ref, dtb_ref, ah_ref,
                dskip_ref, ng_ref, tri_ref, e1_ref, e2_ref, o_ref, xpad_ref, bcpad_ref, state_ref, *, chunk,
                sequence_start):
    halo = 8

    if sequence_start is not False:
        @pl.when(sequence_start)
        def _():
            state_ref[...] = jnp.zeros_like(state_ref)
            xpad_ref[0:halo, :] = jnp.zeros((halo, xpad_ref.shape[1]), F32)
            bcpad_ref[0:halo, :] = jnp.zeros((halo, bcpad_ref.shape[1]), F32)

    xpad_ref[halo:halo + chunk, :] = x_ref[...].astype(F32)
    bcpad_ref[halo:halo + chunk, :] = bc_ref[...]

    def conv(pad_ref, w_ref, b_ref):
        acc = b_ref[...]
        for k in range(SSD_CONV):
            off = halo - (SSD_CONV - 1) + k
            acc = acc + w_ref[k:k + 1, :] * pad_ref[off:off + chunk, :]
        return _silu(acc)

    xs = conv(xpad_ref, cwx_ref, cbx_ref)
    bc = conv(bcpad_ref, cwbc_ref, cbbc_ref)
    xpad_ref[0:halo, :] = xpad_ref[chunk:chunk + halo, :]
    bcpad_ref[0:halo, :] = bcpad_ref[chunk:chunk + halo, :]
    bm = bc[:, :LANES]
    cm = bc[:, LANES:]
    yield

    dt = _softplus(dt_ref[...] + dtb_ref[...])
    da = dt * ah_ref[...]
    acs = _dot_exact_lhs(tri_ref[...], da)
    e1 = e1_ref[...]
    dt_e = _dot_exact_rhs(dt, e1, 2)
    acs_hi, acs_lo = _split(acs, 2)
    acs = acs_hi.astype(F32) + acs_lo.astype(F32)
    acs_e = _dot(acs_hi, e1) + _dot(acs_lo, e1)
    e2 = e2_ref[...]
    acs_e2 = _dot(acs_hi, e2) + _dot(acs_lo, e2)
    acs_t = acs.T

    x_dt = xs * dt_e
    last = acs_e[chunk - 1:chunk, :]
    xs_dec = (x_dt * jnp.exp(last - acs_e)).astype(BF16)

    lane = lax.broadcasted_iota(jnp.int32, (chunk, LANES), 1)
    first = lane < HEAD
    row = lax.broadcasted_iota(jnp.int32, (chunk, chunk), 0)
    col = lax.broadcasted_iota(jnp.int32, (chunk, chunk), 1)
    causal = col <= row
    bm16 = bm.astype(BF16)
    cb = [_dot_nt(jnp.where(first, cm, 0.0).astype(BF16), bm16),
          _dot_nt(jnp.where(first, 0.0, cm).astype(BF16), bm16)]
    x_dt16 = x_dt.astype(BF16)
    yield

    heads_per_group = SSD_HEADS // SSD_GROUPS
    pieces = []
    for pair in range(SSD_HEADS // 2):
        xp = x_dt16[:, pair * LANES:(pair + 1) * LANES]
        yd = []
        for h in (2 * pair, 2 * pair + 1):
            seg = acs_e2[:, h * LANES:(h + 1) * LANES] - acs_t[h:h + 1, :]
            m = jnp.where(causal, jnp.exp(seg), 0.0) * cb[h // heads_per_group]
            yd.append(_dot(m.astype(BF16), xp))
        pieces.append(jnp.where(first, yd[0], yd[1]))
        yield
    y_diag = jnp.concatenate(pieces, axis=1)

    state = state_ref[...]
    y_off = _dot(cm.astype(BF16), state.astype(BF16)) * jnp.exp(acs_e)
    srow = lax.broadcasted_iota(jnp.int32, state.shape, 0)
    slane = lax.broadcasted_iota(jnp.int32, state.shape, 1)
    same_group = (srow < SSD_STATE) == (slane < SSD_WIDTH // SSD_GROUPS)
    state_ref[...] = state * jnp.exp(last) + jnp.where(same_group, _dot_tn(bm16, xs_dec), 0.0)
    yield

    y = y_diag + y_off + xs * dskip_ref[...]
    y = y * _silu(z_ref[...].astype(F32))
    o_ref[...] = _rms_rows(y, ng_ref[...]).astype(BF16)


SSD_BC = 2 * SSD_GROUPS * SSD_STATE
SSD_CHUNK = 128


def _ssd_constants(conv_w, conv_b, dt_bias, a_log, d_skip, norm_g):
    pad16 = lambda v: jnp.pad(v, (0, LANES - SSD_HEADS)).reshape(1, LANES)
    idx = jnp.arange(SSD_CHUNK)
    tri = (idx[None, :] <= idx[:, None]).astype(BF16)
    hidx = jnp.arange(LANES)[:, None]
    e1 = (jnp.arange(SSD_WIDTH)[None, :] // HEAD == hidx).astype(BF16)
    e2 = (jnp.arange(SSD_HEADS * LANES)[None, :] // LANES == hidx).astype(BF16)
    return [conv_w[:, :SSD_WIDTH], conv_w[:, SSD_WIDTH:],
            conv_b[:SSD_WIDTH].reshape(1, -1), conv_b[SSD_WIDTH:].reshape(1, -1),
            pad16(dt_bias), pad16(-jnp.exp(a_log)),
            jnp.repeat(d_skip, HEAD).reshape(1, -1), norm_g.reshape(1, -1),
            tri, e1, e2]


def _merge_kernel(x_ref, ysb_ref, yssd_ref, yrw_ref, gates_ref, wsb_ref, wssd_ref, wrw_ref, wo_ref, fg_ref,
                  o_ref, *, final_norm):
    g = _sigmoid(gates_ref[...].astype(F32))
    merged = (g[:, :D_MODEL] * _dot(ysb_ref[...], wsb_ref[...])
              + g[:, D_MODEL:2 * D_MODEL] * _dot(yssd_ref[...], wssd_ref[...])
              + g[:, 2 * D_MODEL:] * _dot(yrw_ref[...], wrw_ref[...]))
    out = x_ref[...] + _dot(merged.astype(BF16), wo_ref[...])
    if final_norm:
        out = _rms_rows(out, fg_ref[...])
    o_ref[...] = out


def _merge(x2, y_sb, y_ssd, y_rw, proj, w_sb, w_ssd, w_rw, w_o, final_g, *, final_norm, tm=512):
    t, d = x2.shape
    rows = lambda width, colblk=0: pl.BlockSpec((tm, width), lambda i: (i, colblk))
    full = lambda shape: pl.BlockSpec(shape, lambda i: (0, 0))
    return pl.pallas_call(
        functools.partial(_merge_kernel, final_norm=final_norm),
        out_shape=jax.ShapeDtypeStruct((t, d), F32),
        grid=(t // tm,),
        in_specs=[
            rows(d), rows(SB_WIDTH), rows(SSD_WIDTH), rows(RW_WIDTH),
            rows(3 * d, COL_GATES // (3 * d)),
            full((SB_WIDTH, d)), full((SSD_WIDTH, d)), full((RW_WIDTH, d)), full((d, d)), full((1, d)),
        ],
        out_specs=rows(d),
        compiler_params=pltpu.CompilerParams(
            dimension_semantics=("parallel",), vmem_limit_bytes=VMEM_LIMIT),
        name="merge",
    )(x2, y_sb, y_ssd, y_rw, proj, w_sb, w_ssd, w_rw, w_o, final_g.reshape(1, d))


def _dot2(a, b, dims=None):
    a_hi, a_lo = _split(a, 2)
    b_hi, b_lo = _split(b, 2)
    f = _dot if dims is None else dims
    return f(a_hi, b_hi) + f(a_hi, b_lo) + f(a_lo, b_hi)


def _rwkv_kernel(rkvg_ref, lora_ref, mu_ref, mul_ref, w0_ref, wup_ref, a0_ref, aup_ref, kk_ref, ka_ref, rk_ref,
                 lng_ref, lnb_ref, tri_ref, hsum_ref, o_ref, pad_ref, padl_ref, state_ref, y_ref, *, rows, chunk, group):
    halo = 8

    @pl.when(pl.program_id(1) == 0)
    def _():
        state_ref[...] = jnp.zeros_like(state_ref)
        pad_ref[0:halo, :] = jnp.zeros((halo, pad_ref.shape[1]), F32)
        padl_ref[0:halo, :] = jnp.zeros((halo, padl_ref.shape[1]), F32)

    def shift_mix(ref, pad, mu):
        cur = ref[...].astype(F32)
        pad[halo:halo + rows, :] = cur
        prev = pad[halo - 1:halo - 1 + rows, :]
        pad[0:halo, :] = pad[rows:rows + halo, :]
        return cur + (prev - cur) * mu[...]

    mixed = shift_mix(rkvg_ref, pad_ref, mu_ref)
    lora = shift_mix(lora_ref, padl_ref, mul_ref)
    w = RW_WIDTH
    r, k, v, gate = (mixed[:, i * w:(i + 1) * w] for i in range(4))
    hsum = hsum_ref[...]

    def head_sums(x):
        width = hsum.shape[0]
        return jnp.concatenate([_dot_exact_rhs(x[:, i:i + width], hsum, 2) for i in range(0, RW_WIDTH, width)],
                               axis=1)

    w_raw = -_softplus(-(w0_ref[...] + _dot2(jnp.tanh(lora), wup_ref[...]))) - 0.5
    log_w = -jnp.exp(w_raw)
    a = _sigmoid(a0_ref[...] + _dot2(lora, aup_ref[...]))
    kk = k * kk_ref[...]
    kk = kk * lax.rsqrt(jnp.maximum(head_sums(kk * kk), 1e-24))
    k = k * (1.0 + (a - 1.0) * ka_ref[...])
    bonus = head_sums(r * k * rk_ref[...]) * v
    b = kk * a
    yield

    lane = lax.broadcasted_iota(jnp.int32, (chunk, LANES), 1)
    first = lane < HEAD
    n = 2 * chunk
    srow = lax.broadcasted_iota(jnp.int32, (n, n), 0)
    scol = lax.broadcasted_iota(jnp.int32, (n, n), 1)
    strict = scol < srow
    causal = scol <= srow
    eye = (srow == scol).astype(F32)
    drow = lax.broadcasted_iota(jnp.int32, (LANES, LANES), 0)
    dcol = lax.broadcasted_iota(jnp.int32, (LANES, LANES), 1)
    diag = drow == dcol

    nc = rows // chunk
    pairs = RW_WIDTH // LANES
    cl = _dot_exact_lhs(tri_ref[...], log_w)
    cl3 = cl.reshape(nc, chunk, RW_WIDTH)
    last3 = cl3[:, chunk - 1:chunk, :]
    c_tail = jnp.exp(last3 - cl3).reshape(rows, RW_WIDTH)
    c_inv = jnp.exp(-cl)
    c_last = jnp.exp(last3)

    def bmm(x, y, lhs_dim=2, rhs_dim=1):
        dims = (((lhs_dim,), (rhs_dim,)), ((0,), (0,)))
        return lax.dot_general(x.astype(BF16), y.astype(BF16), dims, preferred_element_type=F32)

    def products(c_lo, c_hi):
        def stacks(x):
            x = x.astype(BF16)
            none = jnp.zeros((chunk, LANES), BF16)
            out = []
            for c in range(c_lo, c_hi):
                for p in range(pairs):
                    blk = x[c * chunk:(c + 1) * chunk, p * LANES:(p + 1) * LANES]
                    out.append(jnp.concatenate([jnp.where(first, blk, none), jnp.where(first, none, blk)], axis=0))
            return jnp.stack(out)

        a_s = stacks(-kk * jnp.exp(cl - log_w))
        r_s = stacks(r * jnp.exp(cl))
        b_s = stacks(b * c_inv)
        k_s = stacks(k * c_inv)
        bh_s = stacks(b * c_tail)
        kh_s = stacks(k * c_tail)
        v_s = stacks(v)
        scores = bmm(jnp.concatenate([a_s, r_s], axis=1), jnp.concatenate([b_s, k_s], axis=1), 2, 2)
        a_ab = jnp.where(strict, scores[:, :n, :n], 0.0)
        a_ak = jnp.where(strict, scores[:, :n, n:], 0.0)
        a_rb = jnp.where(causal, scores[:, n:, :n], 0.0).astype(BF16)
        a_rk = jnp.where(causal, scores[:, n:, n:], 0.0)
        yield
        t = eye + a_ab
        x = a_ab.astype(BF16)
        x = bmm(x, x).astype(BF16)
        power = 2
        while 2 * power < chunk:
            both = bmm(jnp.concatenate([t.astype(BF16), x], axis=1), x)
            t = t + both[:, :n]
            x = both[:, n:].astype(BF16)
            power *= 2
            yield
        t = t + bmm(t, x)
        yield
        akv = bmm(jnp.concatenate([a_ak.astype(BF16), a_rk.astype(BF16)], axis=1), v_s)
        pq1 = bmm(t, jnp.concatenate([a_s, akv[:, :n].astype(BF16)], axis=2)).astype(BF16)
        yield
        pq2 = bmm(a_rb, pq1)
        p2 = (r_s.astype(F32) + pq2[:, :, :LANES]).astype(BF16)
        q2 = pq2[:, :, LANES:] + akv[:, n:]
        yield
        gh = bmm(bh_s, pq1, 1, 1)
        return p2, q2, gh[:, :, :LANES], gh[:, :, LANES:] + bmm(kh_s, v_s, 1, 1)

    groups = [(c, min(c + group, nc)) for c in range(0, nc, group)]
    prods = []
    for g in groups:
        prods.append((yield from products(*g)))
    for (c_lo, c_hi), (p2, q2, g_low, h) in zip(groups, prods):
        for c in range(c_lo, c_hi):
            rs = slice(c * chunk, (c + 1) * chunk)
            for p in range(pairs):
                ls = slice(p * LANES, (p + 1) * LANES)
                i = (c - c_lo) * pairs + p
                st = state_ref[p]
                st16 = st.astype(BF16)
                g = jnp.where(diag, c_last[c, :, ls], 0.0) + g_low[i]
                both = _dot(jnp.concatenate([p2[i], g.astype(BF16)], axis=0), st16)
                ys = both[:n] + q2[i]
                y_ref[rs, ls] = ys[:chunk] + ys[chunk:]
                state_ref[p] = both[n:] + h[i]
            yield

    y = y_ref[...]
    inv = 1.0 / HEAD
    mean = head_sums(y) * inv
    d = y - mean
    var = head_sums(d * d) * inv
    y = d * lax.rsqrt(var + GN_EPS) * lng_ref[...] + lnb_ref[...] + bonus
    o_ref[...] = (y * _silu(gate)).astype(BF16)


RW_ROWS = 256
RW_CHUNK = 64


def _rwkv_constants(mu, w0, w_up, a0, a_up, k_k, k_a, r_k, ln_g, ln_b):
    w4 = 4 * RW_WIDTH
    half = RW_LORA // 2
    idx = jnp.arange(RW_ROWS)
    tri = ((idx[None, :] <= idx[:, None]) & (idx[None, :] // RW_CHUNK == idx[:, None] // RW_CHUNK)).astype(BF16)
    lanes = jnp.arange(MXU_TILE)
    hsum = (lanes[:, None] // HEAD == lanes[None, :] // HEAD).astype(BF16)
    wup_pad = jnp.concatenate([w_up, jnp.zeros((half, RW_WIDTH), F32)], axis=0)
    aup_pad = jnp.concatenate([jnp.zeros((half, RW_WIDTH), F32), a_up], axis=0)
    vec = lambda a: a.reshape(1, -1)
    return [vec(mu[:w4]), vec(mu[w4:]), vec(w0), wup_pad, vec(a0), aup_pad, vec(k_k), vec(k_a),
            vec(r_k), vec(ln_g), vec(ln_b), tri, hsum]


def _recurrent_kernel(*refs, n_ssd, n_rw):
    ssd_in, rw_in = refs[:n_ssd], refs[n_ssd:n_ssd + n_rw]
    o_ssd, o_rw = refs[n_ssd + n_rw:n_ssd + n_rw + 2]
    xpad, bcpad, ssd_state, pad, padl, rw_state, y = refs[n_ssd + n_rw + 2:]
    start = pl.program_id(1) == 0

    def ssd_chunks():
        for c in range(RW_ROWS // SSD_CHUNK):
            win = pl.ds(c * SSD_CHUNK, SSD_CHUNK)
            yield from _ssd_kernel(*(r.at[win] for r in ssd_in[:4]), *ssd_in[4:], o_ssd.at[win], xpad, bcpad,
                                   ssd_state, chunk=SSD_CHUNK, sequence_start=start if c == 0 else False)

    stages = [_rwkv_kernel(*rw_in, o_rw, pad, padl, rw_state, y, rows=RW_ROWS, chunk=RW_CHUNK,
                           group=RW_ROWS // RW_CHUNK), ssd_chunks()]
    while stages:
        for body in list(stages):
            if next(body, stages) is stages:
                stages.remove(body)


def _recurrent_mixers(wide, narrow, batch, seq, ssd_params, rw_params):
    nb = seq // RW_ROWS
    w4 = 4 * RW_WIDTH
    ssd_consts = _ssd_constants(*ssd_params)
    rw_consts = _rwkv_constants(*rw_params)
    rows_blk = lambda width, colblk: pl.BlockSpec((RW_ROWS, width), lambda b, c: (b * nb + c, colblk))
    full = lambda a: pl.BlockSpec(a.shape, lambda b, c: (0,) * a.ndim)
    in_specs = ([rows_blk(SSD_WIDTH, COL_Z // SSD_WIDTH), rows_blk(SSD_WIDTH, COL_X // SSD_WIDTH),
                 rows_blk(SSD_BC, COL_BC // SSD_BC), rows_blk(LANES, COL_DT // LANES)]
                + [full(a) for a in ssd_consts]
                + [rows_blk(w4, COL_RW // w4), rows_blk(RW_LORA, COL_LORA // RW_LORA)]
                + [full(a) for a in rw_consts])
    return pl.pallas_call(
        functools.partial(_recurrent_kernel, n_ssd=4 + len(ssd_consts), n_rw=2 + len(rw_consts)),
        out_shape=(jax.ShapeDtypeStruct((batch * seq, SSD_WIDTH), BF16),
                   jax.ShapeDtypeStruct((batch * seq, RW_WIDTH), BF16)),
        grid=(batch, nb),
        in_specs=in_specs,
        out_specs=(rows_blk(SSD_WIDTH, 0), rows_blk(RW_WIDTH, 0)),
        scratch_shapes=[
            pltpu.VMEM((SSD_CHUNK + 8, SSD_WIDTH), F32),
            pltpu.VMEM((SSD_CHUNK + 8, SSD_BC), F32),
            pltpu.VMEM((LANES, SSD_WIDTH), F32),
            pltpu.VMEM((RW_ROWS + 8, w4), F32),
            pltpu.VMEM((RW_ROWS + 8, RW_LORA), F32),
            pltpu.VMEM((RW_WIDTH // LANES, LANES, LANES), F32),
            pltpu.VMEM((RW_ROWS, RW_WIDTH), F32),
        ],
        compiler_params=pltpu.CompilerParams(
            dimension_semantics=("parallel", "arbitrary"), vmem_limit_bytes=VMEM_LIMIT),
        name="recurrent_mixers",
    )(wide, wide, narrow, narrow, *ssd_consts, wide, narrow, *rw_consts)


def kernel(x, norm_g, w_in, conv_w, conv_b, dt_bias, a_log, d_skip, ssd_norm_g, rw_mu, rw_w0, rw_w_up, rw_a0,
           rw_a_up, rw_k_k, rw_k_a, rw_r_k, rw_ln_g, rw_ln_b, w_out_sb, w_out_ssd, w_out_rw, w_o, final_g):
    batch, seq, d = x.shape
    depth = w_in.shape[0]
    x2 = x.reshape(batch * seq, d)
    w_wide = jnp.swapaxes(_pack_wide(w_in).astype(BF16), 1, 2).reshape(depth * N_WIDE, d)
    w_narrow = jnp.swapaxes(_pack_narrow(w_in).astype(BF16), 1, 2).reshape(depth * N_NARROW, d)
    for i in range(depth):
        wide, narrow = _inproj(x2, norm_g[i], w_wide, w_narrow, i)
        y_sb = _sb_attention(wide, batch, seq)
        y_ssd, y_rw = _recurrent_mixers(
            wide, narrow, batch, seq,
            (conv_w[i], conv_b[i], dt_bias[i], a_log[i], d_skip[i], ssd_norm_g[i]),
            (rw_mu[i], rw_w0[i], rw_w_up[i], rw_a0[i], rw_a_up[i], rw_k_k[i], rw_k_a[i], rw_r_k[i].reshape(-1),
             rw_ln_g[i], rw_ln_b[i]))
        x2 = _merge(x2, y_sb, y_ssd, y_rw, wide, w_out_sb[i].astype(BF16), w_out_ssd[i].astype(BF16),
                    w_out_rw[i].astype(BF16), w_o[i].astype(BF16), final_g, final_norm=(i == depth - 1))
    return x2.reshape(batch, seq, d)
```

```python
import functools

import jax
import jax.numpy as jnp
from jax import lax
from jax.experimental import pallas as pl
from jax.experimental.pallas import tpu as pltpu

F32 = jnp.float32
BF16 = jnp.bfloat16

D_MODEL = 1024
HEAD = 64
LANES = 128
MXU_TILE = 256
SB_WIDTH = 512
SSD_WIDTH = 1024
SSD_HEADS = 16
SSD_GROUPS = 2
SSD_STATE = 64
SSD_CONV = 4
RW_WIDTH = 512
RW_LORA = 128
RMS_EPS = 1e-6
GN_EPS = 64e-5

COL_SB = 0
COL_RW = 2048
COL_Z = 4096
COL_X = 5120
COL_GATES = 6144
N_WIDE = 9216
COL_BC = 0
COL_LORA = 256
COL_DT = 384
N_NARROW = 512

VMEM_LIMIT = 48 * 1024 * 1024


def _dot(a, b):
    return jnp.dot(a, b, preferred_element_type=F32)


def _dot_nt(a, b):
    return lax.dot_general(a, b, (((1,), (1,)), ((), ())), preferred_element_type=F32)


def _dot_tn(a, b):
    return lax.dot_general(a, b, (((0,), (0,)), ((), ())), preferred_element_type=F32)


def _split(x, parts):
    out = []
    for _ in range(parts - 1):
        p = x.astype(BF16)
        out.append(p)
        x = x - p.astype(F32)
    out.append(x.astype(BF16))
    return out


def _dot_exact_rhs(a, b_bf16, parts=3):
    acc = None
    for p in _split(a, parts):
        t = _dot(p, b_bf16)
        acc = t if acc is None else acc + t
    return acc


def _dot_exact_lhs(a_bf16, b, parts=3):
    acc = None
    for p in _split(b, parts):
        t = _dot(a_bf16, p)
        acc = t if acc is None else acc + t
    return acc


def _softplus(x):
    return jnp.maximum(x, 0.0) + jnp.log(1.0 + jnp.exp(-jnp.abs(x)))


def _sigmoid(x):
    return 1.0 / (1.0 + jnp.exp(-x))


def _silu(x):
    return x * _sigmoid(x)


def _rms_rows(x, g):
    ms = jnp.mean(x * x, axis=-1, keepdims=True)
    return x * lax.rsqrt(ms + RMS_EPS) * g


def _pack_wide(w):
    return jnp.concatenate([
        w[..., 0:2048],
        w[..., 4368:6416],
        w[..., 2048:3072],
        w[..., 3072:4096],
        w[..., 6544:9616],
    ], axis=-1)


def _pack_narrow(w):
    pad = jnp.zeros(w.shape[:-1] + (N_NARROW - COL_DT - SSD_HEADS,), w.dtype)
    return jnp.concatenate([
        w[..., 4096:4352],
        w[..., 6416:6544],
        w[..., 4352:4368],
        pad], axis=-1)


REPACK_ROWS = 512
WIDE_RUNS = ((0, 0), (2048, 4368), (4096, 2048), (5120, 3072), (6144, 6544))


def _repack_kernel(w_ref, o_ref):
    o_ref[...] = w_ref[0].astype(BF16)


def _repack_wide(w_in_t):
    depth, _, d = w_in_t.shape
    nblk = N_WIDE // REPACK_ROWS

    def source_row(j):
        row = j * REPACK_ROWS
        start = row
        for packed, source in WIDE_RUNS:
            start = jnp.where(row >= packed, source + (row - packed), start)
        return pl.multiple_of(start, 16)

    return pl.pallas_call(
        _repack_kernel,
        out_shape=jax.ShapeDtypeStruct((depth * N_WIDE, d), BF16),
        grid=(depth, nblk),
        in_specs=[pl.BlockSpec((pl.Element(1), pl.Element(REPACK_ROWS), pl.Element(d)),
                               lambda l, j: (l, source_row(j), 0))],
        out_specs=pl.BlockSpec((REPACK_ROWS, d), lambda l, j: (l * nblk + j, 0)),
        compiler_params=pltpu.CompilerParams(
            dimension_semantics=("parallel", "parallel"), vmem_limit_bytes=VMEM_LIMIT),
        name="repack_wide",
    )(w_in_t)


def _inproj_kernel(x_ref, g_ref, w_ref, wn_ref, o_ref, on_ref, h_ref, *, rows):
    @pl.when(pl.program_id(1) == 0)
    def _():
        def body(r, c):
            sl = pl.ds(pl.multiple_of(r * rows, rows), rows)
            h_ref[sl, :] = _rms_rows(x_ref[sl, :], g_ref[...]).astype(BF16)
            return c
        lax.fori_loop(0, x_ref.shape[0] // rows, body, 0)
        on_ref[...] = _dot_nt(h_ref[...], wn_ref[...])

    o_ref[...] = _dot_nt(h_ref[...], w_ref[...]).astype(o_ref.dtype)


def _inproj(x2, g, w_wide, w_narrow, layer, *, tm=1024, tn=1536):
    t, d = x2.shape
    depth = w_narrow.shape[0] // N_NARROW
    n = w_wide.shape[0] // depth
    nn = N_NARROW
    return pl.pallas_call(
        functools.partial(_inproj_kernel, rows=min(tm, 256)),
        out_shape=(jax.ShapeDtypeStruct((t, n), BF16), jax.ShapeDtypeStruct((t, nn), F32)),
        grid=(t // tm, n // tn),
        in_specs=[
            pl.BlockSpec((tm, d), lambda i, j: (i, 0)),
            pl.BlockSpec((1, d), lambda i, j: (0, 0)),
            pl.BlockSpec((tn, d), lambda i, j: (layer * (n // tn) + j, 0)),
            pl.BlockSpec((nn, d), lambda i, j: (layer, 0)),
        ],
        out_specs=(pl.BlockSpec((tm, tn), lambda i, j: (i, j)), pl.BlockSpec((tm, nn), lambda i, j: (i, 0))),
        scratch_shapes=[pltpu.VMEM((tm, d), BF16)],
        compiler_params=pltpu.CompilerParams(
            dimension_semantics=("parallel", "arbitrary"), vmem_limit_bytes=VMEM_LIMIT),
        name="inproj",
    )(x2, g.reshape(1, d), w_wide, w_narrow)


SB_MASKED = -1e30
SB_UNDERFLOW_BITS = 160.0
LOG2_E = 1.4426950408889634


def _sb_kernel(q_ref, k_ref, v_ref, gate_ref, suffix_ref, o_ref, acc_ref, run_ref, *, tile):
    qi = pl.program_id(2)
    rows = 2 * tile
    lane = lax.broadcasted_iota(jnp.int32, (tile, LANES), 1)
    first = lane < HEAD
    suffix = suffix_ref[...]
    row = lax.broadcasted_iota(jnp.int32, (rows, tile), 0)
    col = lax.broadcasted_iota(jnp.int32, (rows, tile), 1)
    strict = col < jnp.where(row >= tile, row - tile, row)

    def q_stack(sub):
        q = q_ref[sub * tile:(sub + 1) * tile, :] * jnp.asarray(HEAD ** -0.5, BF16)
        none = jnp.zeros_like(q)
        return jnp.concatenate([jnp.where(first, q, none), jnp.where(first, none, q)], axis=0)

    def keys(j):
        return pl.ds(pl.multiple_of(j * tile, tile), tile)

    def stage_a(q, j, masked):
        z = _dot_nt(q, k_ref[keys(j), :]) * LOG2_E
        drop = jnp.maximum(z, 0.0) + jnp.log2(1.0 + jnp.exp2(-jnp.abs(z)))
        if masked:
            drop = jnp.where(strict, drop, 0.0)
            z = jnp.where(strict, z, SB_MASKED)
        total = jnp.broadcast_to(jnp.sum(drop, axis=-1, keepdims=True), (rows, LANES))
        return z, drop.astype(BF16), total

    def stage_b(z, drop, before):
        x = z - _dot(drop, suffix)
        if before is not None:
            x = x - jnp.concatenate([before] * (tile // LANES), axis=1)
        return jnp.exp2(x).astype(BF16)

    def stage_c(p, j):
        return _dot(p, v_ref[keys(j), :])

    def near(with_previous):
        q_a, q_b = q_stack(0), q_stack(1)
        j = 2 * qi
        units = [(q_a, j, True), (q_b, j + 1, True), (q_b, j, False)]
        if with_previous:
            units.append((q_a, j - 1, False))
        a = [stage_a(*u) for u in units]
        before = [None, None, a[1][2], a[0][2]]
        p = [stage_b(a[i][0], a[i][1], before[i]) for i in range(len(units))]
        c = [stage_c(p[i], units[i][1]) for i in range(len(units))]
        acc_ref[rows:, :] = c[1] + c[2]
        run_ref[rows:, :] = a[1][2] + a[2][2]
        if with_previous:
            acc_ref[:rows, :] = c[0] + c[3]
            run_ref[:rows, :] = a[0][2] + a[3][2]
        else:
            acc_ref[:rows, :] = c[0]
            run_ref[:rows, :] = a[0][2]

    @pl.when(qi > 0)
    def _():
        near(True)

    @pl.when(qi == 0)
    def _():
        near(False)

    def far(sub, first_tile):
        span = pl.ds(sub * rows, rows)
        q = q_stack(sub)

        def unfinished(c):
            j, nearest = c
            return jnp.logical_and(j >= 0, nearest < SB_UNDERFLOW_BITS)

        def body(c):
            j, _ = c
            z, drop, total = stage_a(q, j, False)
            run = run_ref[span, :]
            acc_ref[span, :] += stage_c(stage_b(z, drop, run), j)
            run = run + total
            run_ref[span, :] = run
            return j - 1, jnp.min(run)
        lax.while_loop(unfinished, body, (first_tile, jnp.min(run_ref[span, :])))

    @pl.when(jnp.logical_and(qi > 0, jnp.min(run_ref[...]) < SB_UNDERFLOW_BITS))
    def _():
        far(0, 2 * qi - 2)
        far(1, 2 * qi - 1)

    acc = acc_ref[...]
    y = jnp.concatenate([jnp.where(first, acc[:tile], acc[tile:rows]),
                         jnp.where(first, acc[rows:rows + tile], acc[rows + tile:])], axis=0)
    o_ref[...] = (y * _silu(gate_ref[...].astype(F32))).astype(BF16)


def _sb_attention(proj, batch, seq, *, tile=256):
    tq = 2 * tile
    nq = seq // tq
    pairs = SB_WIDTH // LANES
    c0 = COL_SB // LANES
    idx = jnp.arange(tile)
    suffix = (idx[:, None] >= idx[None, :]).astype(BF16)
    return pl.pallas_call(
        functools.partial(_sb_kernel, tile=tile),
        out_shape=jax.ShapeDtypeStruct((batch * seq, SB_WIDTH), BF16),
        grid=(batch, pairs, nq),
        in_specs=[
            pl.BlockSpec((tq, LANES), lambda b, p, i: (b * nq + i, c0 + p)),
            pl.BlockSpec((seq, LANES), lambda b, p, i: (b, c0 + pairs + p)),
            pl.BlockSpec((seq, LANES), lambda b, p, i: (b, c0 + 2 * pairs + p)),
            pl.BlockSpec((tq, LANES), lambda b, p, i: (b * nq + i, c0 + 3 * pairs + p)),
            pl.BlockSpec((tile, tile), lambda b, p, i: (0, 0)),
        ],
        out_specs=pl.BlockSpec((tq, LANES), lambda b, p, i: (b * nq + i, p)),
        scratch_shapes=[
            pltpu.VMEM((4 * tile, LANES), F32),
            pltpu.VMEM((4 * tile, LANES), F32),
        ],
        compiler_params=pltpu.CompilerParams(
            dimension_semantics=("parallel", "parallel", "arbitrary"), vmem_limit_bytes=VMEM_LIMIT),
        name="sb_attention",
    )(proj, proj, proj, proj, suffix)


def _ssd_kernel(z_ref, x_ref, bc_ref, dt_ref, cwx_ref, cwbc_ref, cbx_ref, cbbc_ref, dtb_ref, ah_ref,
                dskip_ref, ng_ref, tri_ref, e1_ref, e2_ref, o_ref, xpad_ref, bcpad_ref, state_ref, *, chunk,
                sequence_start):
    halo = 8

    if sequence_start is not False:
        @pl.when(sequence_start)
        def _():
            state_ref[...] = jnp.zeros_like(state_ref)
            xpad_ref[0:halo, :] = jnp.zeros((halo, xpad_ref.shape[1]), F32)
            bcpad_ref[0:halo, :] = jnp.zeros((halo, bcpad_ref.shape[1]), F32)

    xpad_ref[halo:halo + chunk, :] = x_ref[...].astype(F32)
    bcpad_ref[halo:halo + chunk, :] = bc_ref[...]

    def conv(pad_ref, w_ref, b_ref):
        acc = b_ref[...]
        for k in range(SSD_CONV):
            off = halo - (SSD_CONV - 1) + k
            acc = acc + w_ref[k:k + 1, :] * pad_ref[off:off + chunk, :]
        return _silu(acc)

    xs = conv(xpad_ref, cwx_ref, cbx_ref)
    bc = conv(bcpad_ref, cwbc_ref, cbbc_ref)
    xpad_ref[0:halo, :] = xpad_ref[chunk:chunk + halo, :]
    bcpad_ref[0:halo, :] = bcpad_ref[chunk:chunk + halo, :]
    bm = bc[:, :LANES]
    cm = bc[:, LANES:]
    yield

    dt = _softplus(dt_ref[...] + dtb_ref[...])
    da = dt * ah_ref[...]
    acs = _dot_exact_lhs(tri_ref[...], da)
    e1 = e1_ref[...]
    dt_e = _dot_exact_rhs(dt, e1, 2)
    acs_hi, acs_lo = _split(acs, 2)
    acs = acs_hi.astype(F32) + acs_lo.astype(F32)
    acs_e = _dot(acs_hi, e1) + _dot(acs_lo, e1)
    e2 = e2_ref[...]
    acs_e2 = _dot(acs_hi, e2) + _dot(acs_lo, e2)
    acs_t = acs.T

    x_dt = xs * dt_e
    last = acs_e[chunk - 1:chunk, :]
    xs_dec = (x_dt * jnp.exp(last - acs_e)).astype(BF16)

    lane = lax.broadcasted_iota(jnp.int32, (chunk, LANES), 1)
    first = lane < HEAD
    row = lax.broadcasted_iota(jnp.int32, (chunk, chunk), 0)
    col = lax.broadcasted_iota(jnp.int32, (chunk, chunk), 1)
    causal = col <= row
    bm16 = bm.astype(BF16)
    cb = [_dot_nt(jnp.where(first, cm, 0.0).astype(BF16), bm16),
          _dot_nt(jnp.where(first, 0.0, cm).astype(BF16), bm16)]
    x_dt16 = x_dt.astype(BF16)
    yield

    heads_per_group = SSD_HEADS // SSD_GROUPS
    pieces = []
    for pair in range(SSD_HEADS // 2):
        xp = x_dt16[:, pair * LANES:(pair + 1) * LANES]
        yd = []
        for h in (2 * pair, 2 * pair + 1):
            seg = acs_e2[:, h * LANES:(h + 1) * LANES] - acs_t[h:h + 1, :]
            m = jnp.where(causal, jnp.exp(seg), 0.0) * cb[h // heads_per_group]
            yd.append(_dot(m.astype(BF16), xp))
        pieces.append(jnp.where(first, yd[0], yd[1]))
        yield
    y_diag = jnp.concatenate(pieces, axis=1)

    state = state_ref[...]
    y_off = _dot(cm.astype(BF16), state.astype(BF16)) * jnp.exp(acs_e)
    srow = lax.broadcasted_iota(jnp.int32, state.shape, 0)
    slane = lax.broadcasted_iota(jnp.int32, state.shape, 1)
    same_group = (srow < SSD_STATE) == (slane < SSD_WIDTH // SSD_GROUPS)
    state_ref[...] = state * jnp.exp(last) + jnp.where(same_group, _dot_tn(bm16, xs_dec), 0.0)
    yield

    y = y_diag + y_off + xs * dskip_ref[...]
    y = y * _silu(z_ref[...].astype(F32))
    o_ref[...] = _rms_rows(y, ng_ref[...]).astype(BF16)


SSD_BC = 2 * SSD_GROUPS * SSD_STATE
SSD_CHUNK = 128


def _ssd_constants(conv_w, conv_b, dt_bias, a_log, d_skip, norm_g):
    pad16 = lambda v: jnp.pad(v, (0, LANES - SSD_HEADS)).reshape(1, LANES)
    idx = jnp.arange(SSD_CHUNK)
    tri = (idx[None, :] <= idx[:, None]).astype(BF16)
    hidx = jnp.arange(LANES)[:, None]
    e1 = (jnp.arange(SSD_WIDTH)[None, :] // HEAD == hidx).astype(BF16)
    e2 = (jnp.arange(SSD_HEADS * LANES)[None, :] // LANES == hidx).astype(BF16)
    return [conv_w[:, :SSD_WIDTH], conv_w[:, SSD_WIDTH:],
            conv_b[:SSD_WIDTH].reshape(1, -1), conv_b[SSD_WIDTH:].reshape(1, -1),
            pad16(dt_bias), pad16(-jnp.exp(a_log)),
            jnp.repeat(d_skip, HEAD).reshape(1, -1), norm_g.reshape(1, -1),
            tri, e1, e2]


def _merge_kernel(x_ref, ysb_ref, yssd_ref, yrw_ref, gates_ref, wsb_ref, wssd_ref, wrw_ref, wo_ref, fg_ref,
                  o_ref, *, final_norm):
    g = _sigmoid(gates_ref[...].astype(F32))
    merged = (g[:, :D_MODEL] * _dot(ysb_ref[...], wsb_ref[...])
              + g[:, D_MODEL:2 * D_MODEL] * _dot(yssd_ref[...], wssd_ref[...])
              + g[:, 2 * D_MODEL:] * _dot(yrw_ref[...], wrw_ref[...]))
    out = x_ref[...] + _dot(merged.astype(BF16), wo_ref[...])
    if final_norm:
        out = _rms_rows(out, fg_ref[...])
    o_ref[...] = out


def _merge(x2, y_sb, y_ssd, y_rw, proj, w_sb, w_ssd, w_rw, w_o, final_g, *, final_norm, tm=512):
    t, d = x2.shape
    rows = lambda width, colblk=0: pl.BlockSpec((tm, width), lambda i: (i, colblk))
    full = lambda shape: pl.BlockSpec(shape, lambda i: (0, 0))
    return pl.pallas_call(
        functools.partial(_merge_kernel, final_norm=final_norm),
        out_shape=jax.ShapeDtypeStruct((t, d), F32),
        grid=(t // tm,),
        in_specs=[
            rows(d), rows(SB_WIDTH), rows(SSD_WIDTH), rows(RW_WIDTH),
            rows(3 * d, COL_GATES // (3 * d)),
            full((SB_WIDTH, d)), full((SSD_WIDTH, d)), full((RW_WIDTH, d)), full((d, d)), full((1, d)),
        ],
        out_specs=rows(d),
        compiler_params=pltpu.CompilerParams(
            dimension_semantics=("parallel",), vmem_limit_bytes=VMEM_LIMIT),
        name="merge",
    )(x2, y_sb, y_ssd, y_rw, proj, w_sb, w_ssd, w_rw, w_o, final_g.reshape(1, d))


def _dot2(a, b, dims=None):
    a_hi, a_lo = _split(a, 2)
    b_hi, b_lo = _split(b, 2)
    f = _dot if dims is None else dims
    return f(a_hi, b_hi) + f(a_hi, b_lo) + f(a_lo, b_hi)


def _rwkv_kernel(rkvg_ref, lora_ref, mu_ref, mul_ref, w0_ref, wup_ref, a0_ref, aup_ref, kk_ref, ka_ref, rk_ref,
                 lng_ref, lnb_ref, tri_ref, hsum_ref, o_ref, pad_ref, padl_ref, state_ref, y_ref, *, rows, chunk, group):
    halo = 8

    @pl.when(pl.program_id(1) == 0)
    def _():
        state_ref[...] = jnp.zeros_like(state_ref)
        pad_ref[0:halo, :] = jnp.zeros((halo, pad_ref.shape[1]), F32)
        padl_ref[0:halo, :] = jnp.zeros((halo, padl_ref.shape[1]), F32)

    def shift_mix(ref, pad, mu):
        cur = ref[...].astype(F32)
        pad[halo:halo + rows, :] = cur
        prev = pad[halo - 1:halo - 1 + rows, :]
        pad[0:halo, :] = pad[rows:rows + halo, :]
        return cur + (prev - cur) * mu[...]

    mixed = shift_mix(rkvg_ref, pad_ref, mu_ref)
    lora = shift_mix(lora_ref, padl_ref, mul_ref)
    w = RW_WIDTH
    r, k, v, gate = (mixed[:, i * w:(i + 1) * w] for i in range(4))
    hsum = hsum_ref[...]

    def head_sums(x):
        width = hsum.shape[0]
        return jnp.concatenate([_dot_exact_rhs(x[:, i:i + width], hsum, 2) for i in range(0, RW_WIDTH, width)],
                               axis=1)

    w_raw = -_softplus(-(w0_ref[...] + _dot2(jnp.tanh(lora), wup_ref[...]))) - 0.5
    log_w = -jnp.exp(w_raw)
    a = _sigmoid(a0_ref[...] + _dot2(lora, aup_ref[...]))
    kk = k * kk_ref[...]
    kk = kk * lax.rsqrt(jnp.maximum(head_sums(kk * kk), 1e-24))
    k = k * (1.0 + (a - 1.0) * ka_ref[...])
    bonus = head_sums(r * k * rk_ref[...]) * v
    b = kk * a
    yield

    lane = lax.broadcasted_iota(jnp.int32, (chunk, LANES), 1)
    first = lane < HEAD
    n = 2 * chunk
    srow = lax.broadcasted_iota(jnp.int32, (n, n), 0)
    scol = lax.broadcasted_iota(jnp.int32, (n, n), 1)
    strict = scol < srow
    causal = scol <= srow
    eye = (srow == scol).astype(F32)
    drow = lax.broadcasted_iota(jnp.int32, (LANES, LANES), 0)
    dcol = lax.broadcasted_iota(jnp.int32, (LANES, LANES), 1)
    diag = drow == dcol

    nc = rows // chunk
    pairs = RW_WIDTH // LANES
    cl = _dot_exact_lhs(tri_ref[...], log_w)
    cl3 = cl.reshape(nc, chunk, RW_WIDTH)
    last3 = cl3[:, chunk - 1:chunk, :]
    c_tail = jnp.exp(last3 - cl3).reshape(rows, RW_WIDTH)
    c_inv = jnp.exp(-cl)
    c_last = jnp.exp(last3)

    def bmm(x, y, lhs_dim=2, rhs_dim=1):
        dims = (((lhs_dim,), (rhs_dim,)), ((0,), (0,)))
        return lax.dot_general(x.astype(BF16), y.astype(BF16), dims, preferred_element_type=F32)

    def products(c_lo, c_hi):
        def stacks(x):
            x = x.astype(BF16)
            none = jnp.zeros((chunk, LANES), BF16)
            out = []
            for c in range(c_lo, c_hi):
                for p in range(pairs):
                    blk = x[c * chunk:(c + 1) * chunk, p * LANES:(p + 1) * LANES]
                    out.append(jnp.concatenate([jnp.where(first, blk, none), jnp.where(first, none, blk)], axis=0))
            return jnp.stack(out)

        a_s = stacks(-kk * jnp.exp(cl - log_w))
        r_s = stacks(r * jnp.exp(cl))
        b_s = stacks(b * c_inv)
        k_s = stacks(k * c_inv)
        bh_s = stacks(b * c_tail)
        kh_s = stacks(k * c_tail)
        v_s = stacks(v)
        scores = bmm(jnp.concatenate([a_s, r_s], axis=1), jnp.concatenate([b_s, k_s], axis=1), 2, 2)
        a_ab = jnp.where(strict, scores[:, :n, :n], 0.0)
        a_ak = jnp.where(strict, scores[:, :n, n:], 0.0)
        a_rb = jnp.where(causal, scores[:, n:, :n], 0.0).astype(BF16)
        a_rk = jnp.where(causal, scores[:, n:, n:], 0.0)
        yield
        t = eye + a_ab
        x = a_ab.astype(BF16)
        x = bmm(x, x).astype(BF16)
        power = 2
        while 2 * power < chunk:
            both = bmm(jnp.concatenate([t.astype(BF16), x], axis=1), x)
            t = t + both[:, :n]
            x = both[:, n:].astype(BF16)
            power *= 2
            yield
        t = t + bmm(t, x)
        yield
        akv = bmm(jnp.concatenate([a_ak.astype(BF16), a_rk.astype(BF16)], axis=1), v_s)
        pq1 = bmm(t, jnp.concatenate([a_s, akv[:, :n].astype(BF16)], axis=2)).astype(BF16)
        yield
        pq2 = bmm(a_rb, pq1)
        p2 = (r_s.astype(F32) + pq2[:, :, :LANES]).astype(BF16)
        q2 = pq2[:, :, LANES:] + akv[:, n:]
        yield
        gh = bmm(bh_s, pq1, 1, 1)
        return p2, q2, gh[:, :, :LANES], gh[:, :, LANES:] + bmm(kh_s, v_s, 1, 1)

    groups = [(c, min(c + group, nc)) for c in range(0, nc, group)]
    prods = []
    for g in groups:
        prods.append((yield from products(*g)))
    for (c_lo, c_hi), (p2, q2, g_low, h) in zip(groups, prods):
        for c in range(c_lo, c_hi):
            rs = slice(c * chunk, (c + 1) * chunk)
            for p in range(pairs):
                ls = slice(p * LANES, (p + 1) * LANES)
                i = (c - c_lo) * pairs + p
                st = state_ref[p]
                st16 = st.astype(BF16)
                g = jnp.where(diag, c_last[c, :, ls], 0.0) + g_low[i]
                both = _dot(jnp.concatenate([p2[i], g.astype(BF16)], axis=0), st16)
                ys = both[:n] + q2[i]
                y_ref[rs, ls] = ys[:chunk] + ys[chunk:]
                state_ref[p] = both[n:] + h[i]
            yield

    y = y_ref[...]
    inv = 1.0 / HEAD
    mean = head_sums(y) * inv
    d = y - mean
    var = head_sums(d * d) * inv
    y = d * lax.rsqrt(var + GN_EPS) * lng_ref[...] + lnb_ref[...] + bonus
    o_ref[...] = (y * _silu(gate)).astype(BF16)


RW_ROWS = 256
RW_CHUNK = 64


def _rwkv_constants(mu, w0, w_up, a0, a_up, k_k, k_a, r_k, ln_g, ln_b):
    w4 = 4 * RW_WIDTH
    half = RW_LORA // 2
    idx = jnp.arange(RW_ROWS)
    tri = ((idx[None, :] <= idx[:, None]) & (idx[None, :] // RW_CHUNK == idx[:, None] // RW_CHUNK)).astype(BF16)
    lanes = jnp.arange(MXU_TILE)
    hsum = (lanes[:, None] // HEAD == lanes[None, :] // HEAD).astype(BF16)
    wup_pad = jnp.concatenate([w_up, jnp.zeros((half, RW_WIDTH), F32)], axis=0)
    aup_pad = jnp.concatenate([jnp.zeros((half, RW_WIDTH), F32), a_up], axis=0)
    vec = lambda a: a.reshape(1, -1)
    return [vec(mu[:w4]), vec(mu[w4:]), vec(w0), wup_pad, vec(a0), aup_pad, vec(k_k), vec(k_a),
            vec(r_k), vec(ln_g), vec(ln_b), tri, hsum]


def _recurrent_kernel(*refs, n_ssd, n_rw):
    ssd_in, rw_in = refs[:n_ssd], refs[n_ssd:n_ssd + n_rw]
    o_ssd, o_rw = refs[n_ssd + n_rw:n_ssd + n_rw + 2]
    xpad, bcpad, ssd_state, pad, padl, rw_state, y = refs[n_ssd + n_rw + 2:]
    start = pl.program_id(1) == 0

    def ssd_chunks():
        for c in range(RW_ROWS // SSD_CHUNK):
            win = pl.ds(c * SSD_CHUNK, SSD_CHUNK)
            yield from _ssd_kernel(*(r.at[win] for r in ssd_in[:4]), *ssd_in[4:], o_ssd.at[win], xpad, bcpad,
                                   ssd_state, chunk=SSD_CHUNK, sequence_start=start if c == 0 else False)

    stages = [_rwkv_kernel(*rw_in, o_rw, pad, padl, rw_state, y, rows=RW_ROWS, chunk=RW_CHUNK,
                           group=RW_ROWS // RW_CHUNK), ssd_chunks()]
    while stages:
        for body in list(stages):
            if next(body, stages) is stages:
                stages.remove(body)


def _recurrent_mixers(wide, narrow, batch, seq, ssd_params, rw_params):
    nb = seq // RW_ROWS
    w4 = 4 * RW_WIDTH
    ssd_consts = _ssd_constants(*ssd_params)
    rw_consts = _rwkv_constants(*rw_params)
    rows_blk = lambda width, colblk: pl.BlockSpec((RW_ROWS, width), lambda b, c: (b * nb + c, colblk))
    full = lambda a: pl.BlockSpec(a.shape, lambda b, c: (0,) * a.ndim)
    in_specs = ([rows_blk(SSD_WIDTH, COL_Z // SSD_WIDTH), rows_blk(SSD_WIDTH, COL_X // SSD_WIDTH),
                 rows_blk(SSD_BC, COL_BC // SSD_BC), rows_blk(LANES, COL_DT // LANES)]
                + [full(a) for a in ssd_consts]
                + [rows_blk(w4, COL_RW // w4), rows_blk(RW_LORA, COL_LORA // RW_LORA)]
                + [full(a) for a in rw_consts])
    return pl.pallas_call(
        functools.partial(_recurrent_kernel, n_ssd=4 + len(ssd_consts), n_rw=2 + len(rw_consts)),
        out_shape=(jax.ShapeDtypeStruct((batch * seq, SSD_WIDTH), BF16),
                   jax.ShapeDtypeStruct((batch * seq, RW_WIDTH), BF16)),
        grid=(batch, nb),
        in_specs=in_specs,
        out_specs=(rows_blk(SSD_WIDTH, 0), rows_blk(RW_WIDTH, 0)),
        scratch_shapes=[
            pltpu.VMEM((SSD_CHUNK + 8, SSD_WIDTH), F32),
            pltpu.VMEM((SSD_CHUNK + 8, SSD_BC), F32),
            pltpu.VMEM((LANES, SSD_WIDTH), F32),
            pltpu.VMEM((RW_ROWS + 8, w4), F32),
            pltpu.VMEM((RW_ROWS + 8, RW_LORA), F32),
            pltpu.VMEM((RW_WIDTH // LANES, LANES, LANES), F32),
            pltpu.VMEM((RW_ROWS, RW_WIDTH), F32),
        ],
        compiler_params=pltpu.CompilerParams(
            dimension_semantics=("parallel", "arbitrary"), vmem_limit_bytes=VMEM_LIMIT),
        name="recurrent_mixers",
    )(wide, wide, narrow, narrow, *ssd_consts, wide, narrow, *rw_consts)


def kernel(x, norm_g, w_in, conv_w, conv_b, dt_bias, a_log, d_skip, ssd_norm_g, rw_mu, rw_w0, rw_w_up, rw_a0,
           rw_a_up, rw_k_k, rw_k_a, rw_r_k, rw_ln_g, rw_ln_b, w_out_sb, w_out_ssd, w_out_rw, w_o, final_g):
    batch, seq, d = x.shape
    depth = w_in.shape[0]
    x2 = x.reshape(batch * seq, d)
    w_wide = _repack_wide(jnp.swapaxes(w_in, 1, 2))
    w_narrow = jnp.swapaxes(_pack_narrow(w_in).astype(BF16), 1, 2).reshape(depth * N_NARROW, d)
    for i in range(depth):
        wide, narrow = _inproj(x2, norm_g[i], w_wide, w_narrow, i)
        y_sb = _sb_attention(wide, batch, seq)
        y_ssd, y_rw = _recurrent_mixers(
            wide, narrow, batch, seq,
            (conv_w[i], conv_b[i], dt_bias[i], a_log[i], d_skip[i], ssd_norm_g[i]),
            (rw_mu[i], rw_w0[i], rw_w_up[i], rw_a0[i], rw_a_up[i], rw_k_k[i], rw_k_a[i], rw_r_k[i].reshape(-1),
             rw_ln_g[i], rw_ln_b[i]))
        x2 = _merge(x2, y_sb, y_ssd, y_rw, wide, w_out_sb[i].astype(BF16), w_out_ssd[i].astype(BF16),
                    w_out_rw[i].astype(BF16), w_o[i].astype(BF16), final_g, final_norm=(i == depth - 1))
    return x2.reshape(batch, seq, d)
```

```python
import functools

import jax
import jax.numpy as jnp
from jax import lax
from jax.experimental import pallas as pl
from jax.experimental.pallas import tpu as pltpu

F32 = jnp.float32
BF16 = jnp.bfloat16

D_MODEL = 1024
HEAD = 64
LANES = 128
MXU_TILE = 256
SB_WIDTH = 512
SSD_WIDTH = 1024
SSD_HEADS = 16
SSD_GROUPS = 2
SSD_STATE = 64
SSD_CONV = 4
RW_WIDTH = 512
RW_LORA = 128
RMS_EPS = 1e-6
GN_EPS = 64e-5

COL_SB = 0
COL_RW = 2048
COL_Z = 4096
COL_X = 5120
COL_GATES = 6144
N_WIDE = 9216
COL_BC = 0
COL_LORA = 256
COL_DT = 384
N_NARROW = 512

VMEM_LIMIT = 48 * 1024 * 1024


def _dot(a, b):
    return jnp.dot(a, b, preferred_element_type=F32)


def _dot_nt(a, b):
    return lax.dot_general(a, b, (((1,), (1,)), ((), ())), preferred_element_type=F32)


def _dot_tn(a, b):
    return lax.dot_general(a, b, (((0,), (0,)), ((), ())), preferred_element_type=F32)


def _split(x, parts):
    out = []
    for _ in range(parts - 1):
        p = x.astype(BF16)
        out.append(p)
        x = x - p.astype(F32)
    out.append(x.astype(BF16))
    return out


def _dot_exact_rhs(a, b_bf16, parts=3):
    acc = None
    for p in _split(a, parts):
        t = _dot(p, b_bf16)
        acc = t if acc is None else acc + t
    return acc


def _dot_exact_lhs(a_bf16, b, parts=3):
    acc = None
    for p in _split(b, parts):
        t = _dot(a_bf16, p)
        acc = t if acc is None else acc + t
    return acc


def _softplus(x):
    return jnp.maximum(x, 0.0) + jnp.log(1.0 + jnp.exp(-jnp.abs(x)))


def _sigmoid(x):
    return 1.0 / (1.0 + jnp.exp(-x))


def _silu(x):
    return x * _sigmoid(x)


def _rms_rows(x, g):
    ms = jnp.mean(x * x, axis=-1, keepdims=True)
    return x * lax.rsqrt(ms + RMS_EPS) * g


def _pack_wide(w):
    return jnp.concatenate([
        w[..., 0:2048],
        w[..., 4368:6416],
        w[..., 2048:3072],
        w[..., 3072:4096],
        w[..., 6544:9616],
    ], axis=-1)


def _pack_narrow(w):
    pad = jnp.zeros(w.shape[:-1] + (N_NARROW - COL_DT - SSD_HEADS,), w.dtype)
    return jnp.concatenate([
        w[..., 4096:4352],
        w[..., 6416:6544],
        w[..., 4352:4368],
        pad], axis=-1)


WIDE_RUNS = ((0, 0), (2048, 4368), (4096, 2048), (5120, 3072), (6144, 6544))
NARROW_RUNS = ((COL_BC, 4096), (COL_LORA, 6416), (COL_DT, 4352))


def _repack_kernel(w_ref, o_ref, *, last_rows):
    x = w_ref[0]
    if last_rows < x.shape[0]:
        row = lax.broadcasted_iota(jnp.int32, x.shape, 0)
        is_last = pl.program_id(1) == pl.num_programs(1) - 1
        x = jnp.where(jnp.logical_and(is_last, row >= last_rows), 0.0, x)
    o_ref[...] = x.astype(BF16)


def _repack(w_in_t, runs, n_out, block_rows, last_rows):
    depth, _, d = w_in_t.shape
    nblk = n_out // block_rows

    def source_row(j):
        row = j * block_rows
        start = row
        for packed, source in runs:
            start = jnp.where(row >= packed, source + (row - packed), start)
        return pl.multiple_of(start, 16)

    return pl.pallas_call(
        functools.partial(_repack_kernel, last_rows=last_rows),
        out_shape=jax.ShapeDtypeStruct((depth * n_out, d), BF16),
        grid=(depth, nblk),
        in_specs=[pl.BlockSpec((pl.Element(1), pl.Element(block_rows), pl.Element(d)),
                               lambda l, j: (l, source_row(j), 0))],
        out_specs=pl.BlockSpec((block_rows, d), lambda l, j: (l * nblk + j, 0)),
        compiler_params=pltpu.CompilerParams(
            dimension_semantics=("parallel", "parallel"), vmem_limit_bytes=VMEM_LIMIT),
        name="repack",
    )(w_in_t)


def _inproj_kernel(x_ref, g_ref, w_ref, wn_ref, o_ref, on_ref, h_ref, *, rows):
    @pl.when(pl.program_id(1) == 0)
    def _():
        def body(r, c):
            sl = pl.ds(pl.multiple_of(r * rows, rows), rows)
            h_ref[sl, :] = _rms_rows(x_ref[sl, :], g_ref[...]).astype(BF16)
            return c
        lax.fori_loop(0, x_ref.shape[0] // rows, body, 0)
        on_ref[...] = _dot_nt(h_ref[...], wn_ref[...])

    o_ref[...] = _dot_nt(h_ref[...], w_ref[...]).astype(o_ref.dtype)


def _inproj(x2, g, w_wide, w_narrow, layer, *, tm=1024, tn=1536):
    t, d = x2.shape
    depth = w_narrow.shape[0] // N_NARROW
    n = w_wide.shape[0] // depth
    nn = N_NARROW
    return pl.pallas_call(
        functools.partial(_inproj_kernel, rows=min(tm, 256)),
        out_shape=(jax.ShapeDtypeStruct((t, n), BF16), jax.ShapeDtypeStruct((t, nn), F32)),
        grid=(t // tm, n // tn),
        in_specs=[
            pl.BlockSpec((tm, d), lambda i, j: (i, 0)),
            pl.BlockSpec((1, d), lambda i, j: (0, 0)),
            pl.BlockSpec((tn, d), lambda i, j: (layer * (n // tn) + j, 0)),
            pl.BlockSpec((nn, d), lambda i, j: (layer, 0)),
        ],
        out_specs=(pl.BlockSpec((tm, tn), lambda i, j: (i, j)), pl.BlockSpec((tm, nn), lambda i, j: (i, 0))),
        scratch_shapes=[pltpu.VMEM((tm, d), BF16)],
        compiler_params=pltpu.CompilerParams(
            dimension_semantics=("parallel", "arbitrary"), vmem_limit_bytes=VMEM_LIMIT),
        name="inproj",
    )(x2, g.reshape(1, d), w_wide, w_narrow)


SB_MASKED = -1e30
SB_UNDERFLOW_BITS = 160.0
LOG2_E = 1.4426950408889634


def _sb_kernel(q_ref, k_ref, v_ref, gate_ref, suffix_ref, o_ref, acc_ref, run_ref, *, tile):
    qi = pl.program_id(2)
    rows = 2 * tile
    lane = lax.broadcasted_iota(jnp.int32, (tile, LANES), 1)
    first = lane < HEAD
    suffix = suffix_ref[...]
    row = lax.broadcasted_iota(jnp.int32, (rows, tile), 0)
    col = lax.broadcasted_iota(jnp.int32, (rows, tile), 1)
    strict = col < jnp.where(row >= tile, row - tile, row)

    def q_stack(sub):
        q = q_ref[sub * tile:(sub + 1) * tile, :] * jnp.asarray(HEAD ** -0.5, BF16)
        none = jnp.zeros_like(q)
        return jnp.concatenate([jnp.where(first, q, none), jnp.where(first, none, q)], axis=0)

    def keys(j):
        return pl.ds(pl.multiple_of(j * tile, tile), tile)

    def stage_a(q, j, masked):
        z = _dot_nt(q, k_ref[keys(j), :]) * LOG2_E
        drop = jnp.maximum(z, 0.0) + jnp.log2(1.0 + jnp.exp2(-jnp.abs(z)))
        if masked:
            drop = jnp.where(strict, drop, 0.0)
            z = jnp.where(strict, z, SB_MASKED)
        total = jnp.broadcast_to(jnp.sum(drop, axis=-1, keepdims=True), (rows, LANES))
        return z, drop.astype(BF16), total

    def stage_b(z, drop, before):
        x = z - _dot(drop, suffix)
        if before is not None:
            x = x - jnp.concatenate([before] * (tile // LANES), axis=1)
        return jnp.exp2(x).astype(BF16)

    def stage_c(p, j):
        return _dot(p, v_ref[keys(j), :])

    def near(with_previous):
        q_a, q_b = q_stack(0), q_stack(1)
        j = 2 * qi
        units = [(q_a, j, True), (q_b, j + 1, True), (q_b, j, False)]
        if with_previous:
            units.append((q_a, j - 1, False))
        a = [stage_a(*u) for u in units]
        before = [None, None, a[1][2], a[0][2]]
        p = [stage_b(a[i][0], a[i][1], before[i]) for i in range(len(units))]
        c = [stage_c(p[i], units[i][1]) for i in range(len(units))]
        acc_ref[rows:, :] = c[1] + c[2]
        run_ref[rows:, :] = a[1][2] + a[2][2]
        if with_previous:
            acc_ref[:rows, :] = c[0] + c[3]
            run_ref[:rows, :] = a[0][2] + a[3][2]
        else:
            acc_ref[:rows, :] = c[0]
            run_ref[:rows, :] = a[0][2]

    @pl.when(qi > 0)
    def _():
        near(True)

    @pl.when(qi == 0)
    def _():
        near(False)

    def far(sub, first_tile):
        span = pl.ds(sub * rows, rows)
        q = q_stack(sub)

        def unfinished(c):
            j, nearest = c
            return jnp.logical_and(j >= 0, nearest < SB_UNDERFLOW_BITS)

        def body(c):
            j, _ = c
            z, drop, total = stage_a(q, j, False)
            run = run_ref[span, :]
            acc_ref[span, :] += stage_c(stage_b(z, drop, run), j)
            run = run + total
            run_ref[span, :] = run
            return j - 1, jnp.min(run)
        lax.while_loop(unfinished, body, (first_tile, jnp.min(run_ref[span, :])))

    @pl.when(jnp.logical_and(qi > 0, jnp.min(run_ref[...]) < SB_UNDERFLOW_BITS))
    def _():
        far(0, 2 * qi - 2)
        far(1, 2 * qi - 1)

    acc = acc_ref[...]
    y = jnp.concatenate([jnp.where(first, acc[:tile], acc[tile:rows]),
                         jnp.where(first, acc[rows:rows + tile], acc[rows + tile:])], axis=0)
    o_ref[...] = (y * _silu(gate_ref[...].astype(F32))).astype(BF16)


def _sb_attention(proj, batch, seq, *, tile=256):
    tq = 2 * tile
    nq = seq // tq
    pairs = SB_WIDTH // LANES
    c0 = COL_SB // LANES
    idx = jnp.arange(tile)
    suffix = (idx[:, None] >= idx[None, :]).astype(BF16)
    return pl.pallas_call(
        functools.partial(_sb_kernel, tile=tile),
        out_shape=jax.ShapeDtypeStruct((batch * seq, SB_WIDTH), BF16),
        grid=(batch, pairs, nq),
        in_specs=[
            pl.BlockSpec((tq, LANES), lambda b, p, i: (b * nq + i, c0 + p)),
            pl.BlockSpec((seq, LANES), lambda b, p, i: (b, c0 + pairs + p)),
            pl.BlockSpec((seq, LANES), lambda b, p, i: (b, c0 + 2 * pairs + p)),
            pl.BlockSpec((tq, LANES), lambda b, p, i: (b * nq + i, c0 + 3 * pairs + p)),
            pl.BlockSpec((tile, tile), lambda b, p, i: (0, 0)),
        ],
        out_specs=pl.BlockSpec((tq, LANES), lambda b, p, i: (b * nq + i, p)),
        scratch_shapes=[
            pltpu.VMEM((4 * tile, LANES), F32),
            pltpu.VMEM((4 * tile, LANES), F32),
        ],
        compiler_params=pltpu.CompilerParams(
            dimension_semantics=("parallel", "parallel", "arbitrary"), vmem_limit_bytes=VMEM_LIMIT),
        name="sb_attention",
    )(proj, proj, proj, proj, suffix)


def _ssd_kernel(z_ref, x_ref, bc_ref, dt_ref, cwx_ref, cwbc_ref, cbx_ref, cbbc_ref, dtb_ref, ah_ref,
                dskip_ref, ng_ref, tri_ref, e1_ref, e2_ref, o_ref, xpad_ref, bcpad_ref, state_ref, *, chunk,
                sequence_start):
    halo = 8

    if sequence_start is not False:
        @pl.when(sequence_start)
        def _():
            state_ref[...] = jnp.zeros_like(state_ref)
            xpad_ref[0:halo, :] = jnp.zeros((halo, xpad_ref.shape[1]), F32)
            bcpad_ref[0:halo, :] = jnp.zeros((halo, bcpad_ref.shape[1]), F32)

    xpad_ref[halo:halo + chunk, :] = x_ref[...].astype(F32)
    bcpad_ref[halo:halo + chunk, :] = bc_ref[...]

    def conv(pad_ref, w_ref, b_ref):
        acc = b_ref[...]
        for k in range(SSD_CONV):
            off = halo - (SSD_CONV - 1) + k
            acc = acc + w_ref[k:k + 1, :] * pad_ref[off:off + chunk, :]
        return _silu(acc)

    xs = conv(xpad_ref, cwx_ref, cbx_ref)
    bc = conv(bcpad_ref, cwbc_ref, cbbc_ref)
    xpad_ref[0:halo, :] = xpad_ref[chunk:chunk + halo, :]
    bcpad_ref[0:halo, :] = bcpad_ref[chunk:chunk + halo, :]
    bm = bc[:, :LANES]
    cm = bc[:, LANES:]
    yield

    dt = _softplus(dt_ref[...] + dtb_ref[...])
    da = dt * ah_ref[...]
    acs = _dot_exact_lhs(tri_ref[...], da)
    e1 = e1_ref[...]
    dt_e = _dot_exact_rhs(dt, e1, 2)
    acs_hi, acs_lo = _split(acs, 2)
    acs = acs_hi.astype(F32) + acs_lo.astype(F32)
    acs_e = _dot(acs_hi, e1) + _dot(acs_lo, e1)
    e2 = e2_ref[...]
    acs_e2 = _dot(acs_hi, e2) + _dot(acs_lo, e2)
    acs_t = acs.T

    x_dt = xs * dt_e
    last = acs_e[chunk - 1:chunk, :]
    xs_dec = (x_dt * jnp.exp(last - acs_e)).astype(BF16)

    lane = lax.broadcasted_iota(jnp.int32, (chunk, LANES), 1)
    first = lane < HEAD
    row = lax.broadcasted_iota(jnp.int32, (chunk, chunk), 0)
    col = lax.broadcasted_iota(jnp.int32, (chunk, chunk), 1)
    causal = col <= row
    bm16 = bm.astype(BF16)
    cb = [_dot_nt(jnp.where(first, cm, 0.0).astype(BF16), bm16),
          _dot_nt(jnp.where(first, 0.0, cm).astype(BF16), bm16)]
    x_dt16 = x_dt.astype(BF16)
    yield

    heads_per_group = SSD_HEADS // SSD_GROUPS
    pieces = []
    for pair in range(SSD_HEADS // 2):
        xp = x_dt16[:, pair * LANES:(pair + 1) * LANES]
        yd = []
        for h in (2 * pair, 2 * pair + 1):
            seg = acs_e2[:, h * LANES:(h + 1) * LANES] - acs_t[h:h + 1, :]
            m = jnp.where(causal, jnp.exp(seg), 0.0) * cb[h // heads_per_group]
            yd.append(_dot(m.astype(BF16), xp))
        pieces.append(jnp.where(first, yd[0], yd[1]))
        yield
    y_diag = jnp.concatenate(pieces, axis=1)

    state = state_ref[...]
    y_off = _dot(cm.astype(BF16), state.astype(BF16)) * jnp.exp(acs_e)
    srow = lax.broadcasted_iota(jnp.int32, state.shape, 0)
    slane = lax.broadcasted_iota(jnp.int32, state.shape, 1)
    same_group = (srow < SSD_STATE) == (slane < SSD_WIDTH // SSD_GROUPS)
    state_ref[...] = state * jnp.exp(last) + jnp.where(same_group, _dot_tn(bm16, xs_dec), 0.0)
    yield

    y = y_diag + y_off + xs * dskip_ref[...]
    y = y * _silu(z_ref[...].astype(F32))
    o_ref[...] = _rms_rows(y, ng_ref[...]).astype(BF16)


SSD_BC = 2 * SSD_GROUPS * SSD_STATE
SSD_CHUNK = 128


def _ssd_constants(conv_w, conv_b, dt_bias, a_log, d_skip, norm_g):
    pad16 = lambda v: jnp.pad(v, (0, LANES - SSD_HEADS)).reshape(1, LANES)
    idx = jnp.arange(SSD_CHUNK)
    tri = (idx[None, :] <= idx[:, None]).astype(BF16)
    hidx = jnp.arange(LANES)[:, None]
    e1 = (jnp.arange(SSD_WIDTH)[None, :] // HEAD == hidx).astype(BF16)
    e2 = (jnp.arange(SSD_HEADS * LANES)[None, :] // LANES == hidx).astype(BF16)
    return [conv_w[:, :SSD_WIDTH], conv_w[:, SSD_WIDTH:],
            conv_b[:SSD_WIDTH].reshape(1, -1), conv_b[SSD_WIDTH:].reshape(1, -1),
            pad16(dt_bias), pad16(-jnp.exp(a_log)),
            jnp.repeat(d_skip, HEAD).reshape(1, -1), norm_g.reshape(1, -1),
            tri, e1, e2]


def _merge_kernel(x_ref, ysb_ref, yssd_ref, yrw_ref, gates_ref, wsb_ref, wssd_ref, wrw_ref, wo_ref, fg_ref,
                  o_ref, *, final_norm):
    g = _sigmoid(gates_ref[...].astype(F32))
    merged = (g[:, :D_MODEL] * _dot(ysb_ref[...], wsb_ref[...])
              + g[:, D_MODEL:2 * D_MODEL] * _dot(yssd_ref[...], wssd_ref[...])
              + g[:, 2 * D_MODEL:] * _dot(yrw_ref[...], wrw_ref[...]))
    out = x_ref[...] + _dot(merged.astype(BF16), wo_ref[...])
    if final_norm:
        out = _rms_rows(out, fg_ref[...])
    o_ref[...] = out


def _merge(x2, y_sb, y_ssd, y_rw, proj, w_sb, w_ssd, w_rw, w_o, final_g, *, final_norm, tm=512):
    t, d = x2.shape
    rows = lambda width, colblk=0: pl.BlockSpec((tm, width), lambda i: (i, colblk))
    full = lambda shape: pl.BlockSpec(shape, lambda i: (0, 0))
    return pl.pallas_call(
        functools.partial(_merge_kernel, final_norm=final_norm),
        out_shape=jax.ShapeDtypeStruct((t, d), F32),
        grid=(t // tm,),
        in_specs=[
            rows(d), rows(SB_WIDTH), rows(SSD_WIDTH), rows(RW_WIDTH),
            rows(3 * d, COL_GATES // (3 * d)),
            full((SB_WIDTH, d)), full((SSD_WIDTH, d)), full((RW_WIDTH, d)), full((d, d)), full((1, d)),
        ],
        out_specs=rows(d),
        compiler_params=pltpu.CompilerParams(
            dimension_semantics=("parallel",), vmem_limit_bytes=VMEM_LIMIT),
        name="merge",
    )(x2, y_sb, y_ssd, y_rw, proj, w_sb, w_ssd, w_rw, w_o, final_g.reshape(1, d))


def _dot2(a, b, dims=None):
    a_hi, a_lo = _split(a, 2)
    b_hi, b_lo = _split(b, 2)
    f = _dot if dims is None else dims
    return f(a_hi, b_hi) + f(a_hi, b_lo) + f(a_lo, b_hi)


def _rwkv_kernel(rkvg_ref, lora_ref, mu_ref, mul_ref, w0_ref, wup_ref, a0_ref, aup_ref, kk_ref, ka_ref, rk_ref,
                 lng_ref, lnb_ref, tri_ref, hsum_ref, o_ref, pad_ref, padl_ref, state_ref, y_ref, *, rows, chunk, group):
    halo = 8

    @pl.when(pl.program_id(1) == 0)
    def _():
        state_ref[...] = jnp.zeros_like(state_ref)
        pad_ref[0:halo, :] = jnp.zeros((halo, pad_ref.shape[1]), F32)
        padl_ref[0:halo, :] = jnp.zeros((halo, padl_ref.shape[1]), F32)

    def shift_mix(ref, pad, mu):
        cur = ref[...].astype(F32)
        pad[halo:halo + rows, :] = cur
        prev = pad[halo - 1:halo - 1 + rows, :]
        pad[0:halo, :] = pad[rows:rows + halo, :]
        return cur + (prev - cur) * mu[...]

    mixed = shift_mix(rkvg_ref, pad_ref, mu_ref)
    lora = shift_mix(lora_ref, padl_ref, mul_ref)
    w = RW_WIDTH
    r, k, v, gate = (mixed[:, i * w:(i + 1) * w] for i in range(4))
    hsum = hsum_ref[...]

    def head_sums(x):
        width = hsum.shape[0]
        return jnp.concatenate([_dot_exact_rhs(x[:, i:i + width], hsum, 2) for i in range(0, RW_WIDTH, width)],
                               axis=1)

    w_raw = -_softplus(-(w0_ref[...] + _dot2(jnp.tanh(lora), wup_ref[...]))) - 0.5
    log_w = -jnp.exp(w_raw)
    a = _sigmoid(a0_ref[...] + _dot2(lora, aup_ref[...]))
    kk = k * kk_ref[...]
    kk = kk * lax.rsqrt(jnp.maximum(head_sums(kk * kk), 1e-24))
    k = k * (1.0 + (a - 1.0) * ka_ref[...])
    bonus = head_sums(r * k * rk_ref[...]) * v
    b = kk * a
    yield

    lane = lax.broadcasted_iota(jnp.int32, (chunk, LANES), 1)
    first = lane < HEAD
    n = 2 * chunk
    srow = lax.broadcasted_iota(jnp.int32, (n, n), 0)
    scol = lax.broadcasted_iota(jnp.int32, (n, n), 1)
    strict = scol < srow
    causal = scol <= srow
    eye = (srow == scol).astype(F32)
    drow = lax.broadcasted_iota(jnp.int32, (LANES, LANES), 0)
    dcol = lax.broadcasted_iota(jnp.int32, (LANES, LANES), 1)
    diag = drow == dcol

    nc = rows // chunk
    pairs = RW_WIDTH // LANES
    cl = _dot_exact_lhs(tri_ref[...], log_w)
    cl3 = cl.reshape(nc, chunk, RW_WIDTH)
    last3 = cl3[:, chunk - 1:chunk, :]
    c_tail = jnp.exp(last3 - cl3).reshape(rows, RW_WIDTH)
    c_inv = jnp.exp(-cl)
    c_last = jnp.exp(last3)

    def bmm(x, y, lhs_dim=2, rhs_dim=1):
        dims = (((lhs_dim,), (rhs_dim,)), ((0,), (0,)))
        return lax.dot_general(x.astype(BF16), y.astype(BF16), dims, preferred_element_type=F32)

    def products(c_lo, c_hi):
        def stacks(x):
            x = x.astype(BF16)
            none = jnp.zeros((chunk, LANES), BF16)
            out = []
            for c in range(c_lo, c_hi):
                for p in range(pairs):
                    blk = x[c * chunk:(c + 1) * chunk, p * LANES:(p + 1) * LANES]
                    out.append(jnp.concatenate([jnp.where(first, blk, none), jnp.where(first, none, blk)], axis=0))
            return jnp.stack(out)

        a_s = stacks(-kk * jnp.exp(cl - log_w))
        r_s = stacks(r * jnp.exp(cl))
        b_s = stacks(b * c_inv)
        k_s = stacks(k * c_inv)
        bh_s = stacks(b * c_tail)
        kh_s = stacks(k * c_tail)
        v_s = stacks(v)
        scores = bmm(jnp.concatenate([a_s, r_s], axis=1), jnp.concatenate([b_s, k_s], axis=1), 2, 2)
        a_ab = jnp.where(strict, scores[:, :n, :n], 0.0)
        a_ak = jnp.where(strict, scores[:, :n, n:], 0.0)
        a_rb = jnp.where(causal, scores[:, n:, :n], 0.0).astype(BF16)
        a_rk = jnp.where(causal, scores[:, n:, n:], 0.0)
        yield
        t = eye + a_ab
        x = a_ab.astype(BF16)
        x = bmm(x, x).astype(BF16)
        power = 2
        while 2 * power < chunk:
            both = bmm(jnp.concatenate([t.astype(BF16), x], axis=1), x)
            t = t + both[:, :n]
            x = both[:, n:].astype(BF16)
            power *= 2
            yield
        t = t + bmm(t, x)
        yield
        akv = bmm(jnp.concatenate([a_ak.astype(BF16), a_rk.astype(BF16)], axis=1), v_s)
        pq1 = bmm(t, jnp.concatenate([a_s, akv[:, :n].astype(BF16)], axis=2)).astype(BF16)
        yield
        pq2 = bmm(a_rb, pq1)
        p2 = (r_s.astype(F32) + pq2[:, :, :LANES]).astype(BF16)
        q2 = pq2[:, :, LANES:] + akv[:, n:]
        yield
        gh = bmm(bh_s, pq1, 1, 1)
        return p2, q2, gh[:, :, :LANES], gh[:, :, LANES:] + bmm(kh_s, v_s, 1, 1)

    groups = [(c, min(c + group, nc)) for c in range(0, nc, group)]
    prods = []
    for g in groups:
        prods.append((yield from products(*g)))
    for (c_lo, c_hi), (p2, q2, g_low, h) in zip(groups, prods):
        for c in range(c_lo, c_hi):
            rs = slice(c * chunk, (c + 1) * chunk)
            for p in range(pairs):
                ls = slice(p * LANES, (p + 1) * LANES)
                i = (c - c_lo) * pairs + p
                st = state_ref[p]
                st16 = st.astype(BF16)
                g = jnp.where(diag, c_last[c, :, ls], 0.0) + g_low[i]
                both = _dot(jnp.concatenate([p2[i], g.astype(BF16)], axis=0), st16)
                ys = both[:n] + q2[i]
                y_ref[rs, ls] = ys[:chunk] + ys[chunk:]
                state_ref[p] = both[n:] + h[i]
            yield

    y = y_ref[...]
    inv = 1.0 / HEAD
    mean = head_sums(y) * inv
    d = y - mean
    var = head_sums(d * d) * inv
    y = d * lax.rsqrt(var + GN_EPS) * lng_ref[...] + lnb_ref[...] + bonus
    o_ref[...] = (y * _silu(gate)).astype(BF16)


RW_ROWS = 256
RW_CHUNK = 64


def _rwkv_constants(mu, w0, w_up, a0, a_up, k_k, k_a, r_k, ln_g, ln_b):
    w4 = 4 * RW_WIDTH
    half = RW_LORA // 2
    idx = jnp.arange(RW_ROWS)
    tri = ((idx[None, :] <= idx[:, None]) & (idx[None, :] // RW_CHUNK == idx[:, None] // RW_CHUNK)).astype(BF16)
    lanes = jnp.arange(MXU_TILE)
    hsum = (lanes[:, None] // HEAD == lanes[None, :] // HEAD).astype(BF16)
    wup_pad = jnp.concatenate([w_up, jnp.zeros((half, RW_WIDTH), F32)], axis=0)
    aup_pad = jnp.concatenate([jnp.zeros((half, RW_WIDTH), F32), a_up], axis=0)
    vec = lambda a: a.reshape(1, -1)
    return [vec(mu[:w4]), vec(mu[w4:]), vec(w0), wup_pad, vec(a0), aup_pad, vec(k_k), vec(k_a),
            vec(r_k), vec(ln_g), vec(ln_b), tri, hsum]


def _recurrent_kernel(*refs, n_ssd, n_rw):
    ssd_in, rw_in = refs[:n_ssd], refs[n_ssd:n_ssd + n_rw]
    o_ssd, o_rw = refs[n_ssd + n_rw:n_ssd + n_rw + 2]
    xpad, bcpad, ssd_state, pad, padl, rw_state, y = refs[n_ssd + n_rw + 2:]
    start = pl.program_id(1) == 0

    def ssd_chunks():
        for c in range(RW_ROWS // SSD_CHUNK):
            win = pl.ds(c * SSD_CHUNK, SSD_CHUNK)
            yield from _ssd_kernel(*(r.at[win] for r in ssd_in[:4]), *ssd_in[4:], o_ssd.at[win], xpad, bcpad,
                                   ssd_state, chunk=SSD_CHUNK, sequence_start=start if c == 0 else False)

    stages = [_rwkv_kernel(*rw_in, o_rw, pad, padl, rw_state, y, rows=RW_ROWS, chunk=RW_CHUNK,
                           group=RW_ROWS // RW_CHUNK), ssd_chunks()]
    while stages:
        for body in list(stages):
            if next(body, stages) is stages:
                stages.remove(body)


def _recurrent_mixers(wide, narrow, batch, seq, ssd_params, rw_params):
    nb = seq // RW_ROWS
    w4 = 4 * RW_WIDTH
    ssd_consts = _ssd_constants(*ssd_params)
    rw_consts = _rwkv_constants(*rw_params)
    rows_blk = lambda width, colblk: pl.BlockSpec((RW_ROWS, width), lambda b, c: (b * nb + c, colblk))
    full = lambda a: pl.BlockSpec(a.shape, lambda b, c: (0,) * a.ndim)
    in_specs = ([rows_blk(SSD_WIDTH, COL_Z // SSD_WIDTH), rows_blk(SSD_WIDTH, COL_X // SSD_WIDTH),
                 rows_blk(SSD_BC, COL_BC // SSD_BC), rows_blk(LANES, COL_DT // LANES)]
                + [full(a) for a in ssd_consts]
                + [rows_blk(w4, COL_RW // w4), rows_blk(RW_LORA, COL_LORA // RW_LORA)]
                + [full(a) for a in rw_consts])
    return pl.pallas_call(
        functools.partial(_recurrent_kernel, n_ssd=4 + len(ssd_consts), n_rw=2 + len(rw_consts)),
        out_shape=(jax.ShapeDtypeStruct((batch * seq, SSD_WIDTH), BF16),
                   jax.ShapeDtypeStruct((batch * seq, RW_WIDTH), BF16)),
        grid=(batch, nb),
        in_specs=in_specs,
        out_specs=(rows_blk(SSD_WIDTH, 0), rows_blk(RW_WIDTH, 0)),
        scratch_shapes=[
            pltpu.VMEM((SSD_CHUNK + 8, SSD_WIDTH), F32),
            pltpu.VMEM((SSD_CHUNK + 8, SSD_BC), F32),
            pltpu.VMEM((LANES, SSD_WIDTH), F32),
            pltpu.VMEM((RW_ROWS + 8, w4), F32),
            pltpu.VMEM((RW_ROWS + 8, RW_LORA), F32),
            pltpu.VMEM((RW_WIDTH // LANES, LANES, LANES), F32),
            pltpu.VMEM((RW_ROWS, RW_WIDTH), F32),
        ],
        compiler_params=pltpu.CompilerParams(
            dimension_semantics=("parallel", "arbitrary"), vmem_limit_bytes=VMEM_LIMIT),
        name="recurrent_mixers",
    )(wide, wide, narrow, narrow, *ssd_consts, wide, narrow, *rw_consts)


def kernel(x, norm_g, w_in, conv_w, conv_b, dt_bias, a_log, d_skip, ssd_norm_g, rw_mu, rw_w0, rw_w_up, rw_a0,
           rw_a_up, rw_k_k, rw_k_a, rw_r_k, rw_ln_g, rw_ln_b, w_out_sb, w_out_ssd, w_out_rw, w_o, final_g):
    batch, seq, d = x.shape
    depth = w_in.shape[0]
    x2 = x.reshape(batch * seq, d)
    w_in_t = jnp.swapaxes(w_in, 1, 2)
    w_wide = _repack(w_in_t, WIDE_RUNS, N_WIDE, 512, 512)
    w_narrow = _repack(w_in_t, NARROW_RUNS, N_NARROW, LANES, SSD_HEADS)
    for i in range(depth):
        wide, narrow = _inproj(x2, norm_g[i], w_wide, w_narrow, i)
        y_sb = _sb_attention(wide, batch, seq)
        y_ssd, y_rw = _recurrent_mixers(
            wide, narrow, batch, seq,
            (conv_w[i], conv_b[i], dt_bias[i], a_log[i], d_skip[i], ssd_norm_g[i]),
            (rw_mu[i], rw_w0[i], rw_w_up[i], rw_a0[i], rw_a_up[i], rw_k_k[i], rw_k_a[i], rw_r_k[i].reshape(-1),
             rw_ln_g[i], rw_ln_b[i]))
        x2 = _merge(x2, y_sb, y_ssd, y_rw, wide, w_out_sb[i].astype(BF16), w_out_ssd[i].astype(BF16),
                    w_out_rw[i].astype(BF16), w_o[i].astype(BF16), final_g, final_norm=(i == depth - 1))
    return x2.reshape(batch, seq, d)
```

```python
import functools

import jax
import jax.numpy as jnp
from jax import lax
from jax.experimental import pallas as pl
from jax.experimental.pallas import tpu as pltpu

F32 = jnp.float32
BF16 = jnp.bfloat16

D_MODEL = 1024
HEAD = 64
LANES = 128
MXU_TILE = 256
SB_WIDTH = 512
SSD_WIDTH = 1024
SSD_HEADS = 16
SSD_GROUPS = 2
SSD_STATE = 64
SSD_CONV = 4
RW_WIDTH = 512
RW_LORA = 128
RMS_EPS = 1e-6
GN_EPS = 64e-5

COL_SB = 0
COL_RW = 2048
COL_Z = 4096
COL_X = 5120
COL_GATES = 6144
N_WIDE = 9216
COL_BC = 0
COL_LORA = 256
COL_DT = 384
N_NARROW = 512

VMEM_LIMIT = 48 * 1024 * 1024


def _dot(a, b):
    return jnp.dot(a, b, preferred_element_type=F32)


def _dot_nt(a, b):
    return lax.dot_general(a, b, (((1,), (1,)), ((), ())), preferred_element_type=F32)


def _dot_tn(a, b):
    return lax.dot_general(a, b, (((0,), (0,)), ((), ())), preferred_element_type=F32)


def _split(x, parts):
    out = []
    for _ in range(parts - 1):
        p = x.astype(BF16)
        out.append(p)
        x = x - p.astype(F32)
    out.append(x.astype(BF16))
    return out


def _dot_exact_rhs(a, b_bf16, parts=3):
    acc = None
    for p in _split(a, parts):
        t = _dot(p, b_bf16)
        acc = t if acc is None else acc + t
    return acc


def _dot_exact_lhs(a_bf16, b, parts=3):
    acc = None
    for p in _split(b, parts):
        t = _dot(a_bf16, p)
        acc = t if acc is None else acc + t
    return acc


def _softplus(x):
    return jnp.maximum(x, 0.0) + jnp.log(1.0 + jnp.exp(-jnp.abs(x)))


def _sigmoid(x):
    return 1.0 / (1.0 + jnp.exp(-x))


def _silu(x):
    return x * _sigmoid(x)


def _rms_rows(x, g):
    ms = jnp.mean(x * x, axis=-1, keepdims=True)
    return x * lax.rsqrt(ms + RMS_EPS) * g


WIDE_RUNS = ((COL_SB, 0),
             (COL_RW, 4368),
             (COL_Z, 2048),
             (COL_X, 3072),
             (COL_GATES, 6544))
NARROW_RUNS = ((COL_BC, 4096),
               (COL_LORA, 6416),
               (COL_DT, 4352))


def _repack_kernel(w_ref, o_ref, *, last_rows):
    x = w_ref[0]
    if last_rows < x.shape[0]:
        row = lax.broadcasted_iota(jnp.int32, x.shape, 0)
        is_last = pl.program_id(1) == pl.num_programs(1) - 1
        x = jnp.where(jnp.logical_and(is_last, row >= last_rows), 0.0, x)
    o_ref[...] = x.astype(BF16)


def _repack(w_in_t, runs, n_out, block_rows, last_rows):
    depth, _, d = w_in_t.shape
    nblk = n_out // block_rows

    def source_row(j):
        row = j * block_rows
        start = row
        for packed, source in runs:
            start = jnp.where(row >= packed, source + (row - packed), start)
        return pl.multiple_of(start, 16)

    return pl.pallas_call(
        functools.partial(_repack_kernel, last_rows=last_rows),
        out_shape=jax.ShapeDtypeStruct((depth * n_out, d), BF16),
        grid=(depth, nblk),
        in_specs=[pl.BlockSpec((pl.Element(1), pl.Element(block_rows), pl.Element(d)),
                               lambda l, j: (l, source_row(j), 0))],
        out_specs=pl.BlockSpec((block_rows, d), lambda l, j: (l * nblk + j, 0)),
        compiler_params=pltpu.CompilerParams(
            dimension_semantics=("parallel", "parallel"), vmem_limit_bytes=VMEM_LIMIT),
        name="repack",
    )(w_in_t)


def _inproj_kernel(x_ref, g_ref, w_ref, wn_ref, o_ref, on_ref, h_ref, *, rows):
    @pl.when(pl.program_id(1) == 0)
    def _():
        def body(r, c):
            sl = pl.ds(pl.multiple_of(r * rows, rows), rows)
            h_ref[sl, :] = _rms_rows(x_ref[sl, :], g_ref[...]).astype(BF16)
            return c
        lax.fori_loop(0, x_ref.shape[0] // rows, body, 0)
        on_ref[...] = _dot_nt(h_ref[...], wn_ref[...])

    o_ref[...] = _dot_nt(h_ref[...], w_ref[...]).astype(o_ref.dtype)


def _inproj(x2, g, w_wide, w_narrow, layer, *, tm=1024, tn=1536):
    t, d = x2.shape
    depth = w_narrow.shape[0] // N_NARROW
    n = w_wide.shape[0] // depth
    nn = N_NARROW
    return pl.pallas_call(
        functools.partial(_inproj_kernel, rows=min(tm, 256)),
        out_shape=(jax.ShapeDtypeStruct((t, n), BF16), jax.ShapeDtypeStruct((t, nn), F32)),
        grid=(t // tm, n // tn),
        in_specs=[
            pl.BlockSpec((tm, d), lambda i, j: (i, 0)),
            pl.BlockSpec((1, d), lambda i, j: (0, 0)),
            pl.BlockSpec((tn, d), lambda i, j: (layer * (n // tn) + j, 0)),
            pl.BlockSpec((nn, d), lambda i, j: (layer, 0)),
        ],
        out_specs=(pl.BlockSpec((tm, tn), lambda i, j: (i, j)), pl.BlockSpec((tm, nn), lambda i, j: (i, 0))),
        scratch_shapes=[pltpu.VMEM((tm, d), BF16)],
        compiler_params=pltpu.CompilerParams(
            dimension_semantics=("parallel", "arbitrary"), vmem_limit_bytes=VMEM_LIMIT),
        name="inproj",
    )(x2, g.reshape(1, d), w_wide, w_narrow)


SB_MASKED = -1e30
SB_UNDERFLOW_BITS = 160.0
LOG2_E = 1.4426950408889634


def _sb_kernel(q_ref, k_ref, v_ref, gate_ref, suffix_ref, o_ref, acc_ref, run_ref, *, tile):
    qi = pl.program_id(2)
    rows = 2 * tile
    lane = lax.broadcasted_iota(jnp.int32, (tile, LANES), 1)
    first = lane < HEAD
    suffix = suffix_ref[...]
    row = lax.broadcasted_iota(jnp.int32, (rows, tile), 0)
    col = lax.broadcasted_iota(jnp.int32, (rows, tile), 1)
    strict = col < jnp.where(row >= tile, row - tile, row)

    def q_stack(sub):
        q = q_ref[sub * tile:(sub + 1) * tile, :] * jnp.asarray(HEAD ** -0.5, BF16)
        none = jnp.zeros_like(q)
        return jnp.concatenate([jnp.where(first, q, none), jnp.where(first, none, q)], axis=0)

    def keys(j):
        return pl.ds(pl.multiple_of(j * tile, tile), tile)

    def stage_a(q, j, masked):
        z = _dot_nt(q, k_ref[keys(j), :]) * LOG2_E
        drop = jnp.maximum(z, 0.0) + jnp.log2(1.0 + jnp.exp2(-jnp.abs(z)))
        if masked:
            drop = jnp.where(strict, drop, 0.0)
            z = jnp.where(strict, z, SB_MASKED)
        total = jnp.broadcast_to(jnp.sum(drop, axis=-1, keepdims=True), (rows, LANES))
        return z, drop.astype(BF16), total

    def stage_b(z, drop, before):
        x = z - _dot(drop, suffix)
        if before is not None:
            x = x - jnp.concatenate([before] * (tile // LANES), axis=1)
        return jnp.exp2(x).astype(BF16)

    def stage_c(p, j):
        return _dot(p, v_ref[keys(j), :])

    def near(with_previous):
        q_a, q_b = q_stack(0), q_stack(1)
        j = 2 * qi
        units = [(q_a, j, True), (q_b, j + 1, True), (q_b, j, False)]
        if with_previous:
            units.append((q_a, j - 1, False))
        a = [stage_a(*u) for u in units]
        before = [None, None, a[1][2], a[0][2]]
        p = [stage_b(a[i][0], a[i][1], before[i]) for i in range(len(units))]
        c = [stage_c(p[i], units[i][1]) for i in range(len(units))]
        acc_ref[rows:, :] = c[1] + c[2]
        run_ref[rows:, :] = a[1][2] + a[2][2]
        if with_previous:
            acc_ref[:rows, :] = c[0] + c[3]
            run_ref[:rows, :] = a[0][2] + a[3][2]
        else:
            acc_ref[:rows, :] = c[0]
            run_ref[:rows, :] = a[0][2]

    @pl.when(qi > 0)
    def _():
        near(True)

    @pl.when(qi == 0)
    def _():
        near(False)

    def far(sub, first_tile):
        span = pl.ds(sub * rows, rows)
        q = q_stack(sub)

        def unfinished(c):
            j, nearest = c
            return jnp.logical_and(j >= 0, nearest < SB_UNDERFLOW_BITS)

        def body(c):
            j, _ = c
            z, drop, total = stage_a(q, j, False)
            run = run_ref[span, :]
            acc_ref[span, :] += stage_c(stage_b(z, drop, run), j)
            run = run + total
            run_ref[span, :] = run
            return j - 1, jnp.min(run)
        lax.while_loop(unfinished, body, (first_tile, jnp.min(run_ref[span, :])))

    @pl.when(jnp.logical_and(qi > 0, jnp.min(run_ref[...]) < SB_UNDERFLOW_BITS))
    def _():
        far(0, 2 * qi - 2)
        far(1, 2 * qi - 1)

    acc = acc_ref[...]
    y = jnp.concatenate([jnp.where(first, acc[:tile], acc[tile:rows]),
                         jnp.where(first, acc[rows:rows + tile], acc[rows + tile:])], axis=0)
    o_ref[...] = (y * _silu(gate_ref[...].astype(F32))).astype(BF16)


def _sb_attention(proj, batch, seq, *, tile=256):
    tq = 2 * tile
    nq = seq // tq
    pairs = SB_WIDTH // LANES
    c0 = COL_SB // LANES
    idx = jnp.arange(tile)
    suffix = (idx[:, None] >= idx[None, :]).astype(BF16)
    return pl.pallas_call(
        functools.partial(_sb_kernel, tile=tile),
        out_shape=jax.ShapeDtypeStruct((batch * seq, SB_WIDTH), BF16),
        grid=(batch, pairs, nq),
        in_specs=[
            pl.BlockSpec((tq, LANES), lambda b, p, i: (b * nq + i, c0 + p)),
            pl.BlockSpec((seq, LANES), lambda b, p, i: (b, c0 + pairs + p)),
            pl.BlockSpec((seq, LANES), lambda b, p, i: (b, c0 + 2 * pairs + p)),
            pl.BlockSpec((tq, LANES), lambda b, p, i: (b * nq + i, c0 + 3 * pairs + p)),
            pl.BlockSpec((tile, tile), lambda b, p, i: (0, 0)),
        ],
        out_specs=pl.BlockSpec((tq, LANES), lambda b, p, i: (b * nq + i, p)),
        scratch_shapes=[
            pltpu.VMEM((4 * tile, LANES), F32),
            pltpu.VMEM((4 * tile, LANES), F32),
        ],
        compiler_params=pltpu.CompilerParams(
            dimension_semantics=("parallel", "parallel", "arbitrary"), vmem_limit_bytes=VMEM_LIMIT),
        name="sb_attention",
    )(proj, proj, proj, proj, suffix)


def _ssd_kernel(z_ref, x_ref, bc_ref, dt_ref, cwx_ref, cwbc_ref, cbx_ref, cbbc_ref, dtb_ref, ah_ref,
                dskip_ref, ng_ref, tri_ref, e1_ref, e2_ref, o_ref, xpad_ref, bcpad_ref, state_ref, *, chunk,
                sequence_start):
    halo = 8

    if sequence_start is not False:
        @pl.when(sequence_start)
        def _():
            state_ref[...] = jnp.zeros_like(state_ref)
            xpad_ref[0:halo, :] = jnp.zeros((halo, xpad_ref.shape[1]), F32)
            bcpad_ref[0:halo, :] = jnp.zeros((halo, bcpad_ref.shape[1]), F32)

    xpad_ref[halo:halo + chunk, :] = x_ref[...].astype(F32)
    bcpad_ref[halo:halo + chunk, :] = bc_ref[...]

    def conv(pad_ref, w_ref, b_ref):
        acc = b_ref[...]
        for k in range(SSD_CONV):
            off = halo - (SSD_CONV - 1) + k
            acc = acc + w_ref[k:k + 1, :] * pad_ref[off:off + chunk, :]
        return _silu(acc)

    xs = conv(xpad_ref, cwx_ref, cbx_ref)
    bc = conv(bcpad_ref, cwbc_ref, cbbc_ref)
    xpad_ref[0:halo, :] = xpad_ref[chunk:chunk + halo, :]
    bcpad_ref[0:halo, :] = bcpad_ref[chunk:chunk + halo, :]
    bm = bc[:, :LANES]
    cm = bc[:, LANES:]
    yield

    dt = _softplus(dt_ref[...] + dtb_ref[...])
    da = dt * ah_ref[...]
    acs = _dot_exact_lhs(tri_ref[...], da)
    e1 = e1_ref[...]
    dt_e = _dot_exact_rhs(dt, e1, 2)
    acs_hi, acs_lo = _split(acs, 2)
    acs = acs_hi.astype(F32) + acs_lo.astype(F32)
    acs_e = _dot(acs_hi, e1) + _dot(acs_lo, e1)
    e2 = e2_ref[...]
    acs_e2 = _dot(acs_hi, e2) + _dot(acs_lo, e2)
    acs_t = acs.T

    x_dt = xs * dt_e
    last = acs_e[chunk - 1:chunk, :]
    xs_dec = (x_dt * jnp.exp(last - acs_e)).astype(BF16)

    lane = lax.broadcasted_iota(jnp.int32, (chunk, LANES), 1)
    first = lane < HEAD
    row = lax.broadcasted_iota(jnp.int32, (chunk, chunk), 0)
    col = lax.broadcasted_iota(jnp.int32, (chunk, chunk), 1)
    causal = col <= row
    bm16 = bm.astype(BF16)
    cb = [_dot_nt(jnp.where(first, cm, 0.0).astype(BF16), bm16),
          _dot_nt(jnp.where(first, 0.0, cm).astype(BF16), bm16)]
    x_dt16 = x_dt.astype(BF16)
    yield

    heads_per_group = SSD_HEADS // SSD_GROUPS
    pieces = []
    for pair in range(SSD_HEADS // 2):
        xp = x_dt16[:, pair * LANES:(pair + 1) * LANES]
        yd = []
        for h in (2 * pair, 2 * pair + 1):
            seg = acs_e2[:, h * LANES:(h + 1) * LANES] - acs_t[h:h + 1, :]
            m = jnp.where(causal, jnp.exp(seg), 0.0) * cb[h // heads_per_group]
            yd.append(_dot(m.astype(BF16), xp))
        pieces.append(jnp.where(first, yd[0], yd[1]))
        yield
    y_diag = jnp.concatenate(pieces, axis=1)

    state = state_ref[...]
    y_off = _dot(cm.astype(BF16), state.astype(BF16)) * jnp.exp(acs_e)
    srow = lax.broadcasted_iota(jnp.int32, state.shape, 0)
    slane = lax.broadcasted_iota(jnp.int32, state.shape, 1)
    same_group = (srow < SSD_STATE) == (slane < SSD_WIDTH // SSD_GROUPS)
    state_ref[...] = state * jnp.exp(last) + jnp.where(same_group, _dot_tn(bm16, xs_dec), 0.0)
    yield

    y = y_diag + y_off + xs * dskip_ref[...]
    y = y * _silu(z_ref[...].astype(F32))
    o_ref[...] = _rms_rows(y, ng_ref[...]).astype(BF16)


SSD_BC = 2 * SSD_GROUPS * SSD_STATE
SSD_CHUNK = 128


def _ssd_constants(conv_w, conv_b, dt_bias, a_log, d_skip, norm_g):
    pad16 = lambda v: jnp.pad(v, (0, LANES - SSD_HEADS)).reshape(1, LANES)
    idx = jnp.arange(SSD_CHUNK)
    tri = (idx[None, :] <= idx[:, None]).astype(BF16)
    hidx = jnp.arange(LANES)[:, None]
    e1 = (jnp.arange(SSD_WIDTH)[None, :] // HEAD == hidx).astype(BF16)
    e2 = (jnp.arange(SSD_HEADS * LANES)[None, :] // LANES == hidx).astype(BF16)
    return [conv_w[:, :SSD_WIDTH], conv_w[:, SSD_WIDTH:],
            conv_b[:SSD_WIDTH].reshape(1, -1), conv_b[SSD_WIDTH:].reshape(1, -1),
            pad16(dt_bias), pad16(-jnp.exp(a_log)),
            jnp.repeat(d_skip, HEAD).reshape(1, -1), norm_g.reshape(1, -1),
            tri, e1, e2]


def _merge_kernel(x_ref, ysb_ref, yssd_ref, yrw_ref, gates_ref, wsb_ref, wssd_ref, wrw_ref, wo_ref, fg_ref,
                  o_ref, *, final_norm):
    g = _sigmoid(gates_ref[...].astype(F32))
    merged = (g[:, :D_MODEL] * _dot(ysb_ref[...], wsb_ref[...])
              + g[:, D_MODEL:2 * D_MODEL] * _dot(yssd_ref[...], wssd_ref[...])
              + g[:, 2 * D_MODEL:] * _dot(yrw_ref[...], wrw_ref[...]))
    out = x_ref[...] + _dot(merged.astype(BF16), wo_ref[...])
    if final_norm:
        out = _rms_rows(out, fg_ref[...])
    o_ref[...] = out


def _merge(x2, y_sb, y_ssd, y_rw, proj, w_sb, w_ssd, w_rw, w_o, final_g, *, final_norm, tm=512):
    t, d = x2.shape
    rows = lambda width, colblk=0: pl.BlockSpec((tm, width), lambda i: (i, colblk))
    full = lambda shape: pl.BlockSpec(shape, lambda i: (0, 0))
    return pl.pallas_call(
        functools.partial(_merge_kernel, final_norm=final_norm),
        out_shape=jax.ShapeDtypeStruct((t, d), F32),
        grid=(t // tm,),
        in_specs=[
            rows(d), rows(SB_WIDTH), rows(SSD_WIDTH), rows(RW_WIDTH),
            rows(3 * d, COL_GATES // (3 * d)),
            full((SB_WIDTH, d)), full((SSD_WIDTH, d)), full((RW_WIDTH, d)), full((d, d)), full((1, d)),
        ],
        out_specs=rows(d),
        compiler_params=pltpu.CompilerParams(
            dimension_semantics=("parallel",), vmem_limit_bytes=VMEM_LIMIT),
        name="merge",
    )(x2, y_sb, y_ssd, y_rw, proj, w_sb, w_ssd, w_rw, w_o, final_g.reshape(1, d))


def _dot2(a, b, dims=None):
    a_hi, a_lo = _split(a, 2)
    b_hi, b_lo = _split(b, 2)
    f = _dot if dims is None else dims
    return f(a_hi, b_hi) + f(a_hi, b_lo) + f(a_lo, b_hi)


def _rwkv_kernel(rkvg_ref, lora_ref, mu_ref, mul_ref, w0_ref, wup_ref, a0_ref, aup_ref, kk_ref, ka_ref, rk_ref,
                 lng_ref, lnb_ref, tri_ref, hsum_ref, o_ref, pad_ref, padl_ref, state_ref, y_ref, *, rows, chunk, group):
    halo = 8

    @pl.when(pl.program_id(1) == 0)
    def _():
        state_ref[...] = jnp.zeros_like(state_ref)
        pad_ref[0:halo, :] = jnp.zeros((halo, pad_ref.shape[1]), F32)
        padl_ref[0:halo, :] = jnp.zeros((halo, padl_ref.shape[1]), F32)

    def shift_mix(ref, pad, mu):
        cur = ref[...].astype(F32)
        pad[halo:halo + rows, :] = cur
        prev = pad[halo - 1:halo - 1 + rows, :]
        pad[0:halo, :] = pad[rows:rows + halo, :]
        return cur + (prev - cur) * mu[...]

    mixed = shift_mix(rkvg_ref, pad_ref, mu_ref)
    lora = shift_mix(lora_ref, padl_ref, mul_ref)
    w = RW_WIDTH
    r, k, v, gate = (mixed[:, i * w:(i + 1) * w] for i in range(4))
    hsum = hsum_ref[...]

    def head_sums(x):
        width = hsum.shape[0]
        return jnp.concatenate([_dot_exact_rhs(x[:, i:i + width], hsum, 2) for i in range(0, RW_WIDTH, width)],
                               axis=1)

    w_raw = -_softplus(-(w0_ref[...] + _dot2(jnp.tanh(lora), wup_ref[...]))) - 0.5
    log_w = -jnp.exp(w_raw)
    a = _sigmoid(a0_ref[...] + _dot2(lora, aup_ref[...]))
    kk = k * kk_ref[...]
    kk = kk * lax.rsqrt(jnp.maximum(head_sums(kk * kk), 1e-24))
    k = k * (1.0 + (a - 1.0) * ka_ref[...])
    bonus = head_sums(r * k * rk_ref[...]) * v
    b = kk * a
    yield

    lane = lax.broadcasted_iota(jnp.int32, (chunk, LANES), 1)
    first = lane < HEAD
    n = 2 * chunk
    srow = lax.broadcasted_iota(jnp.int32, (n, n), 0)
    scol = lax.broadcasted_iota(jnp.int32, (n, n), 1)
    strict = scol < srow
    causal = scol <= srow
    eye = (srow == scol).astype(F32)
    drow = lax.broadcasted_iota(jnp.int32, (LANES, LANES), 0)
    dcol = lax.broadcasted_iota(jnp.int32, (LANES, LANES), 1)
    diag = drow == dcol

    nc = rows // chunk
    pairs = RW_WIDTH // LANES
    cl = _dot_exact_lhs(tri_ref[...], log_w)
    cl3 = cl.reshape(nc, chunk, RW_WIDTH)
    last3 = cl3[:, chunk - 1:chunk, :]
    c_tail = jnp.exp(last3 - cl3).reshape(rows, RW_WIDTH)
    c_inv = jnp.exp(-cl)
    c_last = jnp.exp(last3)

    def bmm(x, y, lhs_dim=2, rhs_dim=1):
        dims = (((lhs_dim,), (rhs_dim,)), ((0,), (0,)))
        return lax.dot_general(x.astype(BF16), y.astype(BF16), dims, preferred_element_type=F32)

    def products(c_lo, c_hi):
        def stacks(x):
            x = x.astype(BF16)
            none = jnp.zeros((chunk, LANES), BF16)
            out = []
            for c in range(c_lo, c_hi):
                for p in range(pairs):
                    blk = x[c * chunk:(c + 1) * chunk, p * LANES:(p + 1) * LANES]
                    out.append(jnp.concatenate([jnp.where(first, blk, none), jnp.where(first, none, blk)], axis=0))
            return jnp.stack(out)

        a_s = stacks(-kk * jnp.exp(cl - log_w))
        r_s = stacks(r * jnp.exp(cl))
        b_s = stacks(b * c_inv)
        k_s = stacks(k * c_inv)
        bh_s = stacks(b * c_tail)
        kh_s = stacks(k * c_tail)
        v_s = stacks(v)
        scores = bmm(jnp.concatenate([a_s, r_s], axis=1), jnp.concatenate([b_s, k_s], axis=1), 2, 2)
        a_ab = jnp.where(strict, scores[:, :n, :n], 0.0)
        a_ak = jnp.where(strict, scores[:, :n, n:], 0.0)
        a_rb = jnp.where(causal, scores[:, n:, :n], 0.0).astype(BF16)
        a_rk = jnp.where(causal, scores[:, n:, n:], 0.0)
        yield
        t = eye + a_ab
        x = a_ab.astype(BF16)
        x = bmm(x, x).astype(BF16)
        power = 2
        while 2 * power < chunk:
            both = bmm(jnp.concatenate([t.astype(BF16), x], axis=1), x)
            t = t + both[:, :n]
            x = both[:, n:].astype(BF16)
            power *= 2
            yield
        t = t + bmm(t, x)
        yield
        akv = bmm(jnp.concatenate([a_ak.astype(BF16), a_rk.astype(BF16)], axis=1), v_s)
        pq1 = bmm(t, jnp.concatenate([a_s, akv[:, :n].astype(BF16)], axis=2)).astype(BF16)
        yield
        pq2 = bmm(a_rb, pq1)
        p2 = (r_s.astype(F32) + pq2[:, :, :LANES]).astype(BF16)
        q2 = pq2[:, :, LANES:] + akv[:, n:]
        yield
        gh = bmm(bh_s, pq1, 1, 1)
        return p2, q2, gh[:, :, :LANES], gh[:, :, LANES:] + bmm(kh_s, v_s, 1, 1)

    groups = [(c, min(c + group, nc)) for c in range(0, nc, group)]
    prods = []
    for g in groups:
        prods.append((yield from products(*g)))
    for (c_lo, c_hi), (p2, q2, g_low, h) in zip(groups, prods):
        for c in range(c_lo, c_hi):
            rs = slice(c * chunk, (c + 1) * chunk)
            for p in range(pairs):
                ls = slice(p * LANES, (p + 1) * LANES)
                i = (c - c_lo) * pairs + p
                st = state_ref[p]
                st16 = st.astype(BF16)
                g = jnp.where(diag, c_last[c, :, ls], 0.0) + g_low[i]
                both = _dot(jnp.concatenate([p2[i], g.astype(BF16)], axis=0), st16)
                ys = both[:n] + q2[i]
                y_ref[rs, ls] = ys[:chunk] + ys[chunk:]
                state_ref[p] = both[n:] + h[i]
            yield

    y = y_ref[...]
    inv = 1.0 / HEAD
    mean = head_sums(y) * inv
    d = y - mean
    var = head_sums(d * d) * inv
    y = d * lax.rsqrt(var + GN_EPS) * lng_ref[...] + lnb_ref[...] + bonus
    o_ref[...] = (y * _silu(gate)).astype(BF16)


RW_ROWS = 256
RW_CHUNK = 64


def _rwkv_constants(mu, w0, w_up, a0, a_up, k_k, k_a, r_k, ln_g, ln_b):
    w4 = 4 * RW_WIDTH
    half = RW_LORA // 2
    idx = jnp.arange(RW_ROWS)
    tri = ((idx[None, :] <= idx[:, None]) & (idx[None, :] // RW_CHUNK == idx[:, None] // RW_CHUNK)).astype(BF16)
    lanes = jnp.arange(MXU_TILE)
    hsum = (lanes[:, None] // HEAD == lanes[None, :] // HEAD).astype(BF16)
    wup_pad = jnp.concatenate([w_up, jnp.zeros((half, RW_WIDTH), F32)], axis=0)
    aup_pad = jnp.concatenate([jnp.zeros((half, RW_WIDTH), F32), a_up], axis=0)
    vec = lambda a: a.reshape(1, -1)
    return [vec(mu[:w4]), vec(mu[w4:]), vec(w0), wup_pad, vec(a0), aup_pad, vec(k_k), vec(k_a),
            vec(r_k), vec(ln_g), vec(ln_b), tri, hsum]


def _recurrent_kernel(*refs, n_ssd, n_rw):
    ssd_in, rw_in = refs[:n_ssd], refs[n_ssd:n_ssd + n_rw]
    o_ssd, o_rw = refs[n_ssd + n_rw:n_ssd + n_rw + 2]
    xpad, bcpad, ssd_state, pad, padl, rw_state, y = refs[n_ssd + n_rw + 2:]
    start = pl.program_id(1) == 0

    def ssd_chunks():
        for c in range(RW_ROWS // SSD_CHUNK):
            win = pl.ds(c * SSD_CHUNK, SSD_CHUNK)
            yield from _ssd_kernel(*(r.at[win] for r in ssd_in[:4]), *ssd_in[4:], o_ssd.at[win], xpad, bcpad,
                                   ssd_state, chunk=SSD_CHUNK, sequence_start=start if c == 0 else False)

    stages = [_rwkv_kernel(*rw_in, o_rw, pad, padl, rw_state, y, rows=RW_ROWS, chunk=RW_CHUNK,
                           group=RW_ROWS // RW_CHUNK), ssd_chunks()]
    while stages:
        for body in list(stages):
            if next(body, stages) is stages:
                stages.remove(body)


def _recurrent_mixers(wide, narrow, batch, seq, ssd_params, rw_params):
    nb = seq // RW_ROWS
    w4 = 4 * RW_WIDTH
    ssd_consts = _ssd_constants(*ssd_params)
    rw_consts = _rwkv_constants(*rw_params)
    rows_blk = lambda width, colblk: pl.BlockSpec((RW_ROWS, width), lambda b, c: (b * nb + c, colblk))
    full = lambda a: pl.BlockSpec(a.shape, lambda b, c: (0,) * a.ndim)
    in_specs = ([rows_blk(SSD_WIDTH, COL_Z // SSD_WIDTH), rows_blk(SSD_WIDTH, COL_X // SSD_WIDTH),
                 rows_blk(SSD_BC, COL_BC // SSD_BC), rows_blk(LANES, COL_DT // LANES)]
                + [full(a) for a in ssd_consts]
                + [rows_blk(w4, COL_RW // w4), rows_blk(RW_LORA, COL_LORA // RW_LORA)]
                + [full(a) for a in rw_consts])
    return pl.pallas_call(
        functools.partial(_recurrent_kernel, n_ssd=4 + len(ssd_consts), n_rw=2 + len(rw_consts)),
        out_shape=(jax.ShapeDtypeStruct((batch * seq, SSD_WIDTH), BF16),
                   jax.ShapeDtypeStruct((batch * seq, RW_WIDTH), BF16)),
        grid=(batch, nb),
        in_specs=in_specs,
        out_specs=(rows_blk(SSD_WIDTH, 0), rows_blk(RW_WIDTH, 0)),
        scratch_shapes=[
            pltpu.VMEM((SSD_CHUNK + 8, SSD_WIDTH), F32),
            pltpu.VMEM((SSD_CHUNK + 8, SSD_BC), F32),
            pltpu.VMEM((LANES, SSD_WIDTH), F32),
            pltpu.VMEM((RW_ROWS + 8, w4), F32),
            pltpu.VMEM((RW_ROWS + 8, RW_LORA), F32),
            pltpu.VMEM((RW_WIDTH // LANES, LANES, LANES), F32),
            pltpu.VMEM((RW_ROWS, RW_WIDTH), F32),
        ],
        compiler_params=pltpu.CompilerParams(
            dimension_semantics=("parallel", "arbitrary"), vmem_limit_bytes=VMEM_LIMIT),
        name="recurrent_mixers",
    )(wide, wide, narrow, narrow, *ssd_consts, wide, narrow, *rw_consts)


def kernel(x, norm_g, w_in, conv_w, conv_b, dt_bias, a_log, d_skip, ssd_norm_g, rw_mu, rw_w0, rw_w_up, rw_a0,
           rw_a_up, rw_k_k, rw_k_a, rw_r_k, rw_ln_g, rw_ln_b, w_out_sb, w_out_ssd, w_out_rw, w_o, final_g):
    batch, seq, d = x.shape
    depth = w_in.shape[0]
    x2 = x.reshape(batch * seq, d)
    w_in_t = jnp.swapaxes(w_in, 1, 2)
    w_wide = _repack(w_in_t, WIDE_RUNS, N_WIDE, 512, 512)
    w_narrow = _repack(w_in_t, NARROW_RUNS, N_NARROW, LANES, SSD_HEADS)
    for i in range(depth):
        wide, narrow = _inproj(x2, norm_g[i], w_wide, w_narrow, i)
        y_sb = _sb_attention(wide, batch, seq)
        y_ssd, y_rw = _recurrent_mixers(
            wide, narrow, batch, seq,
            (conv_w[i], conv_b[i], dt_bias[i], a_log[i], d_skip[i], ssd_norm_g[i]),
            (rw_mu[i], rw_w0[i], rw_w_up[i], rw_a0[i], rw_a_up[i], rw_k_k[i], rw_k_a[i], rw_r_k[i].reshape(-1),
             rw_ln_g[i], rw_ln_b[i]))
        x2 = _merge(x2, y_sb, y_ssd, y_rw, wide, w_out_sb[i].astype(BF16), w_out_ssd[i].astype(BF16),
                    w_out_rw[i].astype(BF16), w_o[i].astype(BF16), final_g, final_norm=(i == depth - 1))
    return x2.reshape(batch, seq, d)
```

```python
import functools

import jax
import jax.numpy as jnp
from jax import lax
from jax.experimental import pallas as pl
from jax.experimental.pallas import tpu as pltpu

F32 = jnp.float32
BF16 = jnp.bfloat16

D_MODEL = 1024
HEAD = 64
LANES = 128
MXU_TILE = 256
SB_WIDTH = 512
SSD_WIDTH = 1024
SSD_HEADS = 16
SSD_GROUPS = 2
SSD_STATE = 64
SSD_CONV = 4
RW_WIDTH = 512
RW_LORA = 128
RMS_EPS = 1e-6
GN_EPS = 64e-5

COL_SB = 0
COL_RW = 2048
COL_Z = 4096
COL_X = 5120
COL_GATES = 6144
N_WIDE = 9216
COL_BC = 0
COL_LORA = 256
COL_DT = 384
N_NARROW = 512

VMEM_LIMIT = 48 * 1024 * 1024


def _dot(a, b):
    return jnp.dot(a, b, preferred_element_type=F32)


def _dot_nt(a, b):
    return lax.dot_general(a, b, (((1,), (1,)), ((), ())), preferred_element_type=F32)


def _dot_tn(a, b):
    return lax.dot_general(a, b, (((0,), (0,)), ((), ())), preferred_element_type=F32)


def _split(x, parts):
    out = []
    for _ in range(parts - 1):
        p = x.astype(BF16)
        out.append(p)
        x = x - p.astype(F32)
    out.append(x.astype(BF16))
    return out


def _dot_exact_rhs(a, b_bf16, parts=3):
    acc = None
    for p in _split(a, parts):
        t = _dot(p, b_bf16)
        acc = t if acc is None else acc + t
    return acc


def _dot_exact_lhs(a_bf16, b, parts=3):
    acc = None
    for p in _split(b, parts):
        t = _dot(a_bf16, p)
        acc = t if acc is None else acc + t
    return acc


def _softplus(x):
    return jnp.maximum(x, 0.0) + jnp.log(1.0 + jnp.exp(-jnp.abs(x)))


def _sigmoid(x):
    return 0.5 * jnp.tanh(0.5 * x) + 0.5


def _silu(x):
    return x * _sigmoid(x)


def _rms_rows(x, g):
    ms = jnp.mean(x * x, axis=-1, keepdims=True)
    return x * lax.rsqrt(ms + RMS_EPS) * g


WIDE_RUNS = ((COL_SB, 0),
             (COL_RW, 4368),
             (COL_Z, 2048),
             (COL_X, 3072),
             (COL_GATES, 6544))
NARROW_RUNS = ((COL_BC, 4096),
               (COL_LORA, 6416),
               (COL_DT, 4352))


def _repack_kernel(w_ref, o_ref, *, last_rows):
    x = w_ref[0]
    if last_rows < x.shape[0]:
        row = lax.broadcasted_iota(jnp.int32, x.shape, 0)
        is_last = pl.program_id(1) == pl.num_programs(1) - 1
        x = jnp.where(jnp.logical_and(is_last, row >= last_rows), 0.0, x)
    o_ref[...] = x.astype(BF16)


def _repack(w_in_t, runs, n_out, block_rows, last_rows):
    depth, _, d = w_in_t.shape
    nblk = n_out // block_rows

    def source_row(j):
        row = j * block_rows
        start = row
        for packed, source in runs:
            start = jnp.where(row >= packed, source + (row - packed), start)
        return pl.multiple_of(start, 16)

    return pl.pallas_call(
        functools.partial(_repack_kernel, last_rows=last_rows),
        out_shape=jax.ShapeDtypeStruct((depth * n_out, d), BF16),
        grid=(depth, nblk),
        in_specs=[pl.BlockSpec((pl.Element(1), pl.Element(block_rows), pl.Element(d)),
                               lambda l, j: (l, source_row(j), 0))],
        out_specs=pl.BlockSpec((block_rows, d), lambda l, j: (l * nblk + j, 0)),
        compiler_params=pltpu.CompilerParams(
            dimension_semantics=("parallel", "parallel"), vmem_limit_bytes=VMEM_LIMIT),
        name="repack",
    )(w_in_t)


def _inproj_kernel(x_ref, g_ref, w_ref, wn_ref, o_ref, on_ref, h_ref, *, rows):
    @pl.when(pl.program_id(1) == 0)
    def _():
        def body(r, c):
            sl = pl.ds(pl.multiple_of(r * rows, rows), rows)
            h_ref[sl, :] = _rms_rows(x_ref[sl, :], g_ref[...]).astype(BF16)
            return c
        lax.fori_loop(0, x_ref.shape[0] // rows, body, 0)
        on_ref[...] = _dot_nt(h_ref[...], wn_ref[...])

    o_ref[...] = _dot_nt(h_ref[...], w_ref[...]).astype(o_ref.dtype)


def _inproj(x2, g, w_wide, w_narrow, layer, *, tm=1024, tn=1536):
    t, d = x2.shape
    depth = w_narrow.shape[0] // N_NARROW
    n = w_wide.shape[0] // depth
    nn = N_NARROW
    return pl.pallas_call(
        functools.partial(_inproj_kernel, rows=min(tm, 256)),
        out_shape=(jax.ShapeDtypeStruct((t, n), BF16), jax.ShapeDtypeStruct((t, nn), F32)),
        grid=(t // tm, n // tn),
        in_specs=[
            pl.BlockSpec((tm, d), lambda i, j: (i, 0)),
            pl.BlockSpec((1, d), lambda i, j: (0, 0)),
            pl.BlockSpec((tn, d), lambda i, j: (layer * (n // tn) + j, 0)),
            pl.BlockSpec((nn, d), lambda i, j: (layer, 0)),
        ],
        out_specs=(pl.BlockSpec((tm, tn), lambda i, j: (i, j)), pl.BlockSpec((tm, nn), lambda i, j: (i, 0))),
        scratch_shapes=[pltpu.VMEM((tm, d), BF16)],
        compiler_params=pltpu.CompilerParams(
            dimension_semantics=("parallel", "arbitrary"), vmem_limit_bytes=VMEM_LIMIT),
        name="inproj",
    )(x2, g.reshape(1, d), w_wide, w_narrow)


SB_MASKED = -1e30
SB_UNDERFLOW_BITS = 160.0
LOG2_E = 1.4426950408889634


def _sb_kernel(q_ref, k_ref, v_ref, gate_ref, suffix_ref, o_ref, acc_ref, run_ref, *, tile):
    qi = pl.program_id(2)
    rows = 2 * tile
    lane = lax.broadcasted_iota(jnp.int32, (tile, LANES), 1)
    first = lane < HEAD
    suffix = suffix_ref[...]
    row = lax.broadcasted_iota(jnp.int32, (rows, tile), 0)
    col = lax.broadcasted_iota(jnp.int32, (rows, tile), 1)
    strict = col < jnp.where(row >= tile, row - tile, row)

    def q_stack(sub):
        q = q_ref[sub * tile:(sub + 1) * tile, :] * jnp.asarray(HEAD ** -0.5, BF16)
        none = jnp.zeros_like(q)
        return jnp.concatenate([jnp.where(first, q, none), jnp.where(first, none, q)], axis=0)

    def keys(j):
        return pl.ds(pl.multiple_of(j * tile, tile), tile)

    def stage_a(q, j, masked):
        z = _dot_nt(q, k_ref[keys(j), :]) * LOG2_E
        drop = jnp.maximum(z, 0.0) + jnp.log2(1.0 + jnp.exp2(-jnp.abs(z)))
        if masked:
            drop = jnp.where(strict, drop, 0.0)
            z = jnp.where(strict, z, SB_MASKED)
        total = jnp.broadcast_to(jnp.sum(drop, axis=-1, keepdims=True), (rows, LANES))
        return z, drop.astype(BF16), total

    def stage_b(z, drop, before):
        x = z - _dot(drop, suffix)
        if before is not None:
            x = x - jnp.concatenate([before] * (tile // LANES), axis=1)
        return jnp.exp2(x).astype(BF16)

    def stage_c(p, j):
        return _dot(p, v_ref[keys(j), :])

    def near(with_previous):
        q_a, q_b = q_stack(0), q_stack(1)
        j = 2 * qi
        units = [(q_a, j, True), (q_b, j + 1, True), (q_b, j, False)]
        if with_previous:
            units.append((q_a, j - 1, False))
        a = [stage_a(*u) for u in units]
        before = [None, None, a[1][2], a[0][2]]
        p = [stage_b(a[i][0], a[i][1], before[i]) for i in range(len(units))]
        c = [stage_c(p[i], units[i][1]) for i in range(len(units))]
        acc_ref[rows:, :] = c[1] + c[2]
        run_ref[rows:, :] = a[1][2] + a[2][2]
        if with_previous:
            acc_ref[:rows, :] = c[0] + c[3]
            run_ref[:rows, :] = a[0][2] + a[3][2]
        else:
            acc_ref[:rows, :] = c[0]
            run_ref[:rows, :] = a[0][2]

    @pl.when(qi > 0)
    def _():
        near(True)

    @pl.when(qi == 0)
    def _():
        near(False)

    def far(sub, first_tile):
        span = pl.ds(sub * rows, rows)
        q = q_stack(sub)

        def unfinished(c):
            j, nearest = c
            return jnp.logical_and(j >= 0, nearest < SB_UNDERFLOW_BITS)

        def body(c):
            j, _ = c
            z, drop, total = stage_a(q, j, False)
            run = run_ref[span, :]
            acc_ref[span, :] += stage_c(stage_b(z, drop, run), j)
            run = run + total
            run_ref[span, :] = run
            return j - 1, jnp.min(run)
        lax.while_loop(unfinished, body, (first_tile, jnp.min(run_ref[span, :])))

    @pl.when(jnp.logical_and(qi > 0, jnp.min(run_ref[...]) < SB_UNDERFLOW_BITS))
    def _():
        far(0, 2 * qi - 2)
        far(1, 2 * qi - 1)

    acc = acc_ref[...]
    y = jnp.concatenate([jnp.where(first, acc[:tile], acc[tile:rows]),
                         jnp.where(first, acc[rows:rows + tile], acc[rows + tile:])], axis=0)
    o_ref[...] = (y * _silu(gate_ref[...].astype(F32))).astype(BF16)


def _sb_attention(proj, batch, seq, *, tile=256):
    tq = 2 * tile
    nq = seq // tq
    pairs = SB_WIDTH // LANES
    c0 = COL_SB // LANES
    idx = jnp.arange(tile)
    suffix = (idx[:, None] >= idx[None, :]).astype(BF16)
    return pl.pallas_call(
        functools.partial(_sb_kernel, tile=tile),
        out_shape=jax.ShapeDtypeStruct((batch * seq, SB_WIDTH), BF16),
        grid=(batch, pairs, nq),
        in_specs=[
            pl.BlockSpec((tq, LANES), lambda b, p, i: (b * nq + i, c0 + p)),
            pl.BlockSpec((seq, LANES), lambda b, p, i: (b, c0 + pairs + p)),
            pl.BlockSpec((seq, LANES), lambda b, p, i: (b, c0 + 2 * pairs + p)),
            pl.BlockSpec((tq, LANES), lambda b, p, i: (b * nq + i, c0 + 3 * pairs + p)),
            pl.BlockSpec((tile, tile), lambda b, p, i: (0, 0)),
        ],
        out_specs=pl.BlockSpec((tq, LANES), lambda b, p, i: (b * nq + i, p)),
        scratch_shapes=[
            pltpu.VMEM((4 * tile, LANES), F32),
            pltpu.VMEM((4 * tile, LANES), F32),
        ],
        compiler_params=pltpu.CompilerParams(
            dimension_semantics=("parallel", "parallel", "arbitrary"), vmem_limit_bytes=VMEM_LIMIT),
        name="sb_attention",
    )(proj, proj, proj, proj, suffix)


def _ssd_kernel(z_ref, x_ref, bc_ref, dt_ref, cwx_ref, cwbc_ref, cbx_ref, cbbc_ref, dtb_ref, ah_ref,
                dskip_ref, ng_ref, tri_ref, e1_ref, e2_ref, o_ref, xpad_ref, bcpad_ref, state_ref, *, chunk,
                sequence_start):
    halo = 8

    if sequence_start is not False:
        @pl.when(sequence_start)
        def _():
            state_ref[...] = jnp.zeros_like(state_ref)
            xpad_ref[0:halo, :] = jnp.zeros((halo, xpad_ref.shape[1]), F32)
            bcpad_ref[0:halo, :] = jnp.zeros((halo, bcpad_ref.shape[1]), F32)

    xpad_ref[halo:halo + chunk, :] = x_ref[...].astype(F32)
    bcpad_ref[halo:halo + chunk, :] = bc_ref[...]

    def conv(pad_ref, w_ref, b_ref):
        acc = b_ref[...]
        for k in range(SSD_CONV):
            off = halo - (SSD_CONV - 1) + k
            acc = acc + w_ref[k:k + 1, :] * pad_ref[off:off + chunk, :]
        return _silu(acc)

    xs = conv(xpad_ref, cwx_ref, cbx_ref)
    bc = conv(bcpad_ref, cwbc_ref, cbbc_ref)
    xpad_ref[0:halo, :] = xpad_ref[chunk:chunk + halo, :]
    bcpad_ref[0:halo, :] = bcpad_ref[chunk:chunk + halo, :]
    bm = bc[:, :LANES]
    cm = bc[:, LANES:]
    yield

    dt = _softplus(dt_ref[...] + dtb_ref[...])
    da = dt * ah_ref[...]
    acs = _dot_exact_lhs(tri_ref[...], da)
    e1 = e1_ref[...]
    dt_e = _dot_exact_rhs(dt, e1, 2)
    acs_hi, acs_lo = _split(acs, 2)
    acs = acs_hi.astype(F32) + acs_lo.astype(F32)
    acs_e = _dot(acs_hi, e1) + _dot(acs_lo, e1)
    e2 = e2_ref[...]
    acs_e2 = _dot(acs_hi, e2) + _dot(acs_lo, e2)
    acs_t = acs.T

    x_dt = xs * dt_e
    last = acs_e[chunk - 1:chunk, :]
    xs_dec = (x_dt * jnp.exp(last - acs_e)).astype(BF16)

    lane = lax.broadcasted_iota(jnp.int32, (chunk, LANES), 1)
    first = lane < HEAD
    row = lax.broadcasted_iota(jnp.int32, (chunk, chunk), 0)
    col = lax.broadcasted_iota(jnp.int32, (chunk, chunk), 1)
    causal = col <= row
    bm16 = bm.astype(BF16)
    cb = [_dot_nt(jnp.where(first, cm, 0.0).astype(BF16), bm16),
          _dot_nt(jnp.where(first, 0.0, cm).astype(BF16), bm16)]
    x_dt16 = x_dt.astype(BF16)
    yield

    heads_per_group = SSD_HEADS // SSD_GROUPS
    pieces = []
    for pair in range(SSD_HEADS // 2):
        xp = x_dt16[:, pair * LANES:(pair + 1) * LANES]
        yd = []
        for h in (2 * pair, 2 * pair + 1):
            seg = acs_e2[:, h * LANES:(h + 1) * LANES] - acs_t[h:h + 1, :]
            m = jnp.where(causal, jnp.exp(seg), 0.0) * cb[h // heads_per_group]
            yd.append(_dot(m.astype(BF16), xp))
        pieces.append(jnp.where(first, yd[0], yd[1]))
        yield
    y_diag = jnp.concatenate(pieces, axis=1)

    state = state_ref[...]
    y_off = _dot(cm.astype(BF16), state.astype(BF16)) * jnp.exp(acs_e)
    srow = lax.broadcasted_iota(jnp.int32, state.shape, 0)
    slane = lax.broadcasted_iota(jnp.int32, state.shape, 1)
    same_group = (srow < SSD_STATE) == (slane < SSD_WIDTH // SSD_GROUPS)
    state_ref[...] = state * jnp.exp(last) + jnp.where(same_group, _dot_tn(bm16, xs_dec), 0.0)
    yield

    y = y_diag + y_off + xs * dskip_ref[...]
    y = y * _silu(z_ref[...].astype(F32))
    o_ref[...] = _rms_rows(y, ng_ref[...]).astype(BF16)


SSD_BC = 2 * SSD_GROUPS * SSD_STATE
SSD_CHUNK = 128


def _ssd_constants(conv_w, conv_b, dt_bias, a_log, d_skip, norm_g):
    pad16 = lambda v: jnp.pad(v, (0, LANES - SSD_HEADS)).reshape(1, LANES)
    idx = jnp.arange(SSD_CHUNK)
    tri = (idx[None, :] <= idx[:, None]).astype(BF16)
    hidx = jnp.arange(LANES)[:, None]
    e1 = (jnp.arange(SSD_WIDTH)[None, :] // HEAD == hidx).astype(BF16)
    e2 = (jnp.arange(SSD_HEADS * LANES)[None, :] // LANES == hidx).astype(BF16)
    return [conv_w[:, :SSD_WIDTH], conv_w[:, SSD_WIDTH:],
            conv_b[:SSD_WIDTH].reshape(1, -1), conv_b[SSD_WIDTH:].reshape(1, -1),
            pad16(dt_bias), pad16(-jnp.exp(a_log)),
            jnp.repeat(d_skip, HEAD).reshape(1, -1), norm_g.reshape(1, -1),
            tri, e1, e2]


def _merge_kernel(x_ref, ysb_ref, yssd_ref, yrw_ref, gates_ref, wsb_ref, wssd_ref, wrw_ref, wo_ref, fg_ref,
                  o_ref, *, final_norm):
    g = _sigmoid(gates_ref[...].astype(F32))
    merged = (g[:, :D_MODEL] * _dot(ysb_ref[...], wsb_ref[...])
              + g[:, D_MODEL:2 * D_MODEL] * _dot(yssd_ref[...], wssd_ref[...])
              + g[:, 2 * D_MODEL:] * _dot(yrw_ref[...], wrw_ref[...]))
    out = x_ref[...] + _dot(merged.astype(BF16), wo_ref[...])
    if final_norm:
        out = _rms_rows(out, fg_ref[...])
    o_ref[...] = out


def _merge(x2, y_sb, y_ssd, y_rw, proj, w_sb, w_ssd, w_rw, w_o, final_g, *, final_norm, tm=512):
    t, d = x2.shape
    rows = lambda width, colblk=0: pl.BlockSpec((tm, width), lambda i: (i, colblk))
    full = lambda shape: pl.BlockSpec(shape, lambda i: (0, 0))
    return pl.pallas_call(
        functools.partial(_merge_kernel, final_norm=final_norm),
        out_shape=jax.ShapeDtypeStruct((t, d), F32),
        grid=(t // tm,),
        in_specs=[
            rows(d), rows(SB_WIDTH), rows(SSD_WIDTH), rows(RW_WIDTH),
            rows(3 * d, COL_GATES // (3 * d)),
            full((SB_WIDTH, d)), full((SSD_WIDTH, d)), full((RW_WIDTH, d)), full((d, d)), full((1, d)),
        ],
        out_specs=rows(d),
        compiler_params=pltpu.CompilerParams(
            dimension_semantics=("parallel",), vmem_limit_bytes=VMEM_LIMIT),
        name="merge",
    )(x2, y_sb, y_ssd, y_rw, proj, w_sb, w_ssd, w_rw, w_o, final_g.reshape(1, d))


def _dot2(a, b, dims=None):
    a_hi, a_lo = _split(a, 2)
    b_hi, b_lo = _split(b, 2)
    f = _dot if dims is None else dims
    return f(a_hi, b_hi) + f(a_hi, b_lo) + f(a_lo, b_hi)


def _rwkv_kernel(rkvg_ref, lora_ref, mu_ref, mul_ref, w0_ref, wup_ref, a0_ref, aup_ref, kk_ref, ka_ref, rk_ref,
                 lng_ref, lnb_ref, tri_ref, hsum_ref, o_ref, pad_ref, padl_ref, state_ref, y_ref, *, rows, chunk, group):
    halo = 8

    @pl.when(pl.program_id(1) == 0)
    def _():
        state_ref[...] = jnp.zeros_like(state_ref)
        pad_ref[0:halo, :] = jnp.zeros((halo, pad_ref.shape[1]), F32)
        padl_ref[0:halo, :] = jnp.zeros((halo, padl_ref.shape[1]), F32)

    def shift_mix(ref, pad, mu):
        cur = ref[...].astype(F32)
        pad[halo:halo + rows, :] = cur
        prev = pad[halo - 1:halo - 1 + rows, :]
        pad[0:halo, :] = pad[rows:rows + halo, :]
        return cur + (prev - cur) * mu[...]

    mixed = shift_mix(rkvg_ref, pad_ref, mu_ref)
    lora = shift_mix(lora_ref, padl_ref, mul_ref)
    w = RW_WIDTH
    r, k, v, gate = (mixed[:, i * w:(i + 1) * w] for i in range(4))
    hsum = hsum_ref[...]

    def head_sums(x):
        width = hsum.shape[0]
        return jnp.concatenate([_dot_exact_rhs(x[:, i:i + width], hsum, 2) for i in range(0, RW_WIDTH, width)],
                               axis=1)

    w_raw = -_softplus(-(w0_ref[...] + _dot2(jnp.tanh(lora), wup_ref[...]))) - 0.5
    log_w = -jnp.exp(w_raw)
    a = _sigmoid(a0_ref[...] + _dot2(lora, aup_ref[...]))
    kk = k * kk_ref[...]
    kk = kk * lax.rsqrt(jnp.maximum(head_sums(kk * kk), 1e-24))
    k = k * (1.0 + (a - 1.0) * ka_ref[...])
    bonus = head_sums(r * k * rk_ref[...]) * v
    b = kk * a
    yield

    lane = lax.broadcasted_iota(jnp.int32, (chunk, LANES), 1)
    first = lane < HEAD
    n = 2 * chunk
    srow = lax.broadcasted_iota(jnp.int32, (n, n), 0)
    scol = lax.broadcasted_iota(jnp.int32, (n, n), 1)
    strict = scol < srow
    causal = scol <= srow
    eye = (srow == scol).astype(F32)
    drow = lax.broadcasted_iota(jnp.int32, (LANES, LANES), 0)
    dcol = lax.broadcasted_iota(jnp.int32, (LANES, LANES), 1)
    diag = drow == dcol

    nc = rows // chunk
    pairs = RW_WIDTH // LANES
    cl = _dot_exact_lhs(tri_ref[...], log_w)
    cl3 = cl.reshape(nc, chunk, RW_WIDTH)
    last3 = cl3[:, chunk - 1:chunk, :]
    c_tail = jnp.exp(last3 - cl3).reshape(rows, RW_WIDTH)
    c_inv = jnp.exp(-cl)
    c_last = jnp.exp(last3)

    def bmm(x, y, lhs_dim=2, rhs_dim=1):
        dims = (((lhs_dim,), (rhs_dim,)), ((0,), (0,)))
        return lax.dot_general(x.astype(BF16), y.astype(BF16), dims, preferred_element_type=F32)

    def products(c_lo, c_hi):
        def stacks(x):
            x = x.astype(BF16)
            none = jnp.zeros((chunk, LANES), BF16)
            out = []
            for c in range(c_lo, c_hi):
                for p in range(pairs):
                    blk = x[c * chunk:(c + 1) * chunk, p * LANES:(p + 1) * LANES]
                    out.append(jnp.concatenate([jnp.where(first, blk, none), jnp.where(first, none, blk)], axis=0))
            return jnp.stack(out)

        a_s = stacks(-kk * jnp.exp(cl - log_w))
        r_s = stacks(r * jnp.exp(cl))
        b_s = stacks(b * c_inv)
        k_s = stacks(k * c_inv)
        bh_s = stacks(b * c_tail)
        kh_s = stacks(k * c_tail)
        v_s = stacks(v)
        scores = bmm(jnp.concatenate([a_s, r_s], axis=1), jnp.concatenate([b_s, k_s], axis=1), 2, 2)
        a_ab = jnp.where(strict, scores[:, :n, :n], 0.0)
        a_ak = jnp.where(strict, scores[:, :n, n:], 0.0)
        a_rb = jnp.where(causal, scores[:, n:, :n], 0.0).astype(BF16)
        a_rk = jnp.where(causal, scores[:, n:, n:], 0.0)
        yield
        t = eye + a_ab
        x = a_ab.astype(BF16)
        x = bmm(x, x).astype(BF16)
        power = 2
        while 2 * power < chunk:
            both = bmm(jnp.concatenate([t.astype(BF16), x], axis=1), x)
            t = t + both[:, :n]
            x = both[:, n:].astype(BF16)
            power *= 2
            yield
        t = t + bmm(t, x)
        yield
        akv = bmm(jnp.concatenate([a_ak.astype(BF16), a_rk.astype(BF16)], axis=1), v_s)
        pq1 = bmm(t, jnp.concatenate([a_s, akv[:, :n].astype(BF16)], axis=2)).astype(BF16)
        yield
        pq2 = bmm(a_rb, pq1)
        p2 = (r_s.astype(F32) + pq2[:, :, :LANES]).astype(BF16)
        q2 = pq2[:, :, LANES:] + akv[:, n:]
        yield
        gh = bmm(bh_s, pq1, 1, 1)
        return p2, q2, gh[:, :, :LANES], gh[:, :, LANES:] + bmm(kh_s, v_s, 1, 1)

    groups = [(c, min(c + group, nc)) for c in range(0, nc, group)]
    prods = []
    for g in groups:
        prods.append((yield from products(*g)))
    for (c_lo, c_hi), (p2, q2, g_low, h) in zip(groups, prods):
        for c in range(c_lo, c_hi):
            rs = slice(c * chunk, (c + 1) * chunk)
            for p in range(pairs):
                ls = slice(p * LANES, (p + 1) * LANES)
                i = (c - c_lo) * pairs + p
                st = state_ref[p]
                st16 = st.astype(BF16)
                g = jnp.where(diag, c_last[c, :, ls], 0.0) + g_low[i]
                both = _dot(jnp.concatenate([p2[i], g.astype(BF16)], axis=0), st16)
                ys = both[:n] + q2[i]
                y_ref[rs, ls] = ys[:chunk] + ys[chunk:]
                state_ref[p] = both[n:] + h[i]
            yield

    y = y_ref[...]
    inv = 1.0 / HEAD
    mean = head_sums(y) * inv
    d = y - mean
    var = head_sums(d * d) * inv
    y = d * lax.rsqrt(var + GN_EPS) * lng_ref[...] + lnb_ref[...] + bonus
    o_ref[...] = (y * _silu(gate)).astype(BF16)


RW_ROWS = 256
RW_CHUNK = 64


def _rwkv_constants(mu, w0, w_up, a0, a_up, k_k, k_a, r_k, ln_g, ln_b):
    w4 = 4 * RW_WIDTH
    half = RW_LORA // 2
    idx = jnp.arange(RW_ROWS)
    tri = ((idx[None, :] <= idx[:, None]) & (idx[None, :] // RW_CHUNK == idx[:, None] // RW_CHUNK)).astype(BF16)
    lanes = jnp.arange(MXU_TILE)
    hsum = (lanes[:, None] // HEAD == lanes[None, :] // HEAD).astype(BF16)
    wup_pad = jnp.concatenate([w_up, jnp.zeros((half, RW_WIDTH), F32)], axis=0)
    aup_pad = jnp.concatenate([jnp.zeros((half, RW_WIDTH), F32), a_up], axis=0)
    vec = lambda a: a.reshape(1, -1)
    return [vec(mu[:w4]), vec(mu[w4:]), vec(w0), wup_pad, vec(a0), aup_pad, vec(k_k), vec(k_a),
            vec(r_k), vec(ln_g), vec(ln_b), tri, hsum]


def _recurrent_kernel(*refs, n_ssd, n_rw):
    ssd_in, rw_in = refs[:n_ssd], refs[n_ssd:n_ssd + n_rw]
    o_ssd, o_rw = refs[n_ssd + n_rw:n_ssd + n_rw + 2]
    xpad, bcpad, ssd_state, pad, padl, rw_state, y = refs[n_ssd + n_rw + 2:]
    start = pl.program_id(1) == 0

    def ssd_chunks():
        for c in range(RW_ROWS // SSD_CHUNK):
            win = pl.ds(c * SSD_CHUNK, SSD_CHUNK)
            yield from _ssd_kernel(*(r.at[win] for r in ssd_in[:4]), *ssd_in[4:], o_ssd.at[win], xpad, bcpad,
                                   ssd_state, chunk=SSD_CHUNK, sequence_start=start if c == 0 else False)

    stages = [_rwkv_kernel(*rw_in, o_rw, pad, padl, rw_state, y, rows=RW_ROWS, chunk=RW_CHUNK,
                           group=RW_ROWS // RW_CHUNK), ssd_chunks()]
    while stages:
        for body in list(stages):
            if next(body, stages) is stages:
                stages.remove(body)


def _recurrent_mixers(wide, narrow, batch, seq, ssd_params, rw_params):
    nb = seq // RW_ROWS
    w4 = 4 * RW_WIDTH
    ssd_consts = _ssd_constants(*ssd_params)
    rw_consts = _rwkv_constants(*rw_params)
    rows_blk = lambda width, colblk: pl.BlockSpec((RW_ROWS, width), lambda b, c: (b * nb + c, colblk))
    full = lambda a: pl.BlockSpec(a.shape, lambda b, c: (0,) * a.ndim)
    in_specs = ([rows_blk(SSD_WIDTH, COL_Z // SSD_WIDTH), rows_blk(SSD_WIDTH, COL_X // SSD_WIDTH),
                 rows_blk(SSD_BC, COL_BC // SSD_BC), rows_blk(LANES, COL_DT // LANES)]
                + [full(a) for a in ssd_consts]
                + [rows_blk(w4, COL_RW // w4), rows_blk(RW_LORA, COL_LORA // RW_LORA)]
                + [full(a) for a in rw_consts])
    return pl.pallas_call(
        functools.partial(_recurrent_kernel, n_ssd=4 + len(ssd_consts), n_rw=2 + len(rw_consts)),
        out_shape=(jax.ShapeDtypeStruct((batch * seq, SSD_WIDTH), BF16),
                   jax.ShapeDtypeStruct((batch * seq, RW_WIDTH), BF16)),
        grid=(batch, nb),
        in_specs=in_specs,
        out_specs=(rows_blk(SSD_WIDTH, 0), rows_blk(RW_WIDTH, 0)),
        scratch_shapes=[
            pltpu.VMEM((SSD_CHUNK + 8, SSD_WIDTH), F32),
            pltpu.VMEM((SSD_CHUNK + 8, SSD_BC), F32),
            pltpu.VMEM((LANES, SSD_WIDTH), F32),
            pltpu.VMEM((RW_ROWS + 8, w4), F32),
            pltpu.VMEM((RW_ROWS + 8, RW_LORA), F32),
            pltpu.VMEM((RW_WIDTH // LANES, LANES, LANES), F32),
            pltpu.VMEM((RW_ROWS, RW_WIDTH), F32),
        ],
        compiler_params=pltpu.CompilerParams(
            dimension_semantics=("parallel", "arbitrary"), vmem_limit_bytes=VMEM_LIMIT),
        name="recurrent_mixers",
    )(wide, wide, narrow, narrow, *ssd_consts, wide, narrow, *rw_consts)


def kernel(x, norm_g, w_in, conv_w, conv_b, dt_bias, a_log, d_skip, ssd_norm_g, rw_mu, rw_w0, rw_w_up, rw_a0,
           rw_a_up, rw_k_k, rw_k_a, rw_r_k, rw_ln_g, rw_ln_b, w_out_sb, w_out_ssd, w_out_rw, w_o, final_g):
    batch, seq, d = x.shape
    depth = w_in.shape[0]
    x2 = x.reshape(batch * seq, d)
    w_in_t = jnp.swapaxes(w_in, 1, 2)
    w_wide = _repack(w_in_t, WIDE_RUNS, N_WIDE, 512, 512)
    w_narrow = _repack(w_in_t, NARROW_RUNS, N_NARROW, LANES, SSD_HEADS)
    for i in range(depth):
        wide, narrow = _inproj(x2, norm_g[i], w_wide, w_narrow, i)
        y_sb = _sb_attention(wide, batch, seq)
        y_ssd, y_rw = _recurrent_mixers(
            wide, narrow, batch, seq,
            (conv_w[i], conv_b[i], dt_bias[i], a_log[i], d_skip[i], ssd_norm_g[i]),
            (rw_mu[i], rw_w0[i], rw_w_up[i], rw_a0[i], rw_a_up[i], rw_k_k[i], rw_k_a[i], rw_r_k[i].reshape(-1),
             rw_ln_g[i], rw_ln_b[i]))
        x2 = _merge(x2, y_sb, y_ssd, y_rw, wide, w_out_sb[i].astype(BF16), w_out_ssd[i].astype(BF16),
                    w_out_rw[i].astype(BF16), w_o[i].astype(BF16), final_g, final_norm=(i == depth - 1))
    return x2.reshape(batch, seq, d)
```
